```python
import jax, jax.numpy as jnp
from jax import lax
import numpy as np

D_MODEL = 1024
BATCH = 8
SEQ = 4096
DEPTH = 1
DEC_BATCH = 128
DEC_SEQ = 4
PAST_LEN = 16384
PAGE_SIZE = 128

MLA_HEADS = 8
MLA_Q_LORA = 256
MLA_KV_LORA = 128
MLA_D_NOPE = 64
MLA_D_ROPE = 32
MLA_D_V = 64
MLA_SCALE = (MLA_D_NOPE + MLA_D_ROPE) ** -0.5
MLA_Q_CHUNK = 128
MOBA_HEADS = 8
MOBA_HEAD_DIM = 64
MOBA_WIDTH = MOBA_HEADS * MOBA_HEAD_DIM
MOBA_BLOCK = 256
MOBA_TOPK = 3
MOBA_Q_CHUNK = 64
MOBA_SCALE = MOBA_HEAD_DIM ** -0.5
MIX_WIDTH = MLA_HEADS * MLA_D_V + MOBA_WIDTH
IN_COLS = MLA_Q_LORA + MLA_KV_LORA + MLA_D_ROPE + 3 * MOBA_WIDTH
IN_SPLITS = (MLA_Q_LORA, MLA_Q_LORA + MLA_KV_LORA, MLA_Q_LORA + MLA_KV_LORA + MLA_D_ROPE,
             MLA_Q_LORA + MLA_KV_LORA + MLA_D_ROPE + MOBA_WIDTH,
             MLA_Q_LORA + MLA_KV_LORA + MLA_D_ROPE + 2 * MOBA_WIDTH)
MOE_GROUPS = 4
MOE_EXPERTS_PER_GROUP = 8
MOE_EXPERTS = MOE_GROUPS * MOE_EXPERTS_PER_GROUP
MOE_TOPK = 2
D_EXPERT = 512
MOE_ROW_BLOCK = 128
PLE_DIM = 256
ROPE_THETA = 10000.0
NORM_EPS = 1e-6

kernel_name = 'hybrid_mla_moba_hmoe_step'


def _rmsnorm(x, g):
    x32 = x.astype(jnp.float32)
    y = x32 * lax.rsqrt(jnp.mean(x32 * x32, axis=-1, keepdims=True) + NORM_EPS)
    return (y * g.astype(jnp.float32)).astype(x.dtype)


def _rope(x, pos):
    d = x.shape[-1]
    half = d // 2
    inv = jnp.power(ROPE_THETA, -jnp.arange(half, dtype=jnp.float32) * (2.0 / d))
    ang = pos.astype(jnp.float32)[:, None] * inv[None, :]
    shp = (ang.shape[0],) + (1,) * (x.ndim - 3) + (half,)
    cos = jnp.cos(ang).reshape(shp)
    sin = jnp.sin(ang).reshape(shp)
    x32 = x.astype(jnp.float32)
    x1, x2 = x32[..., :half], x32[..., half:]
    return jnp.concatenate([x1 * cos - x2 * sin, x1 * sin + x2 * cos], axis=-1).astype(x.dtype)


def _masked_softmax(s, mask):
    return jax.nn.softmax(jnp.where(mask, s.astype(jnp.float32), -jnp.inf), axis=-1)


def _project(x, pos, g_attn, w_in, g_q, w_uq, g_kv, w_uk):
    bsz, t = x.shape[:2]
    h = _rmsnorm(x, g_attn)
    c_q, c_kv, k_pe, q_b, k_b, v_b = jnp.split(h @ w_in, IN_SPLITS, axis=-1)
    q = jnp.einsum('btc,chd->bthd', _rmsnorm(c_q, g_q), w_uq)
    q_pe = _rope(q[..., MLA_D_NOPE:], pos)
    q_lat = jnp.einsum('bthd,chd->bthc', q[..., :MLA_D_NOPE], w_uk)
    ckv = _rmsnorm(c_kv, g_kv)
    kpe = _rope(k_pe, pos)
    shp = (bsz, t, MOBA_HEADS, MOBA_HEAD_DIM)
    q_b = _rope(q_b.reshape(shp), pos)
    k_b = _rope(k_b.reshape(shp), pos)
    v_b = v_b.reshape(shp)
    return q_lat, q_pe, ckv, kpe, q_b, k_b, v_b


def _mla_scores(q_lat, q_pe, ckv, kpe):
    return (jnp.einsum('bthc,bsc->bhts', q_lat, ckv)
            + jnp.einsum('bthr,bsr->bhts', q_pe, kpe)) * MLA_SCALE


def _mla_prompt(q_lat, q_pe, ckv, kpe, w_uv):
    bsz, s = q_lat.shape[:2]
    kpos = jnp.arange(s)

    def one(c):
        q0 = c * MLA_Q_CHUNK
        ql = lax.dynamic_slice_in_dim(q_lat, q0, MLA_Q_CHUNK, axis=1)
        qp = lax.dynamic_slice_in_dim(q_pe, q0, MLA_Q_CHUNK, axis=1)
        qpos = q0 + jnp.arange(MLA_Q_CHUNK)
        p = _masked_softmax(_mla_scores(ql, qp, ckv, kpe), kpos[None, :] <= qpos[:, None])
        return jnp.einsum('bhts,bsc->bthc', p.astype(ckv.dtype), ckv)

    o = lax.map(one, jnp.arange(s // MLA_Q_CHUNK))
    o = jnp.moveaxis(o, 0, 1).reshape(bsz, s, MLA_HEADS, MLA_KV_LORA)
    return jnp.einsum('bthc,chd->bthd', o, w_uv).reshape(bsz, s, MLA_HEADS * MLA_D_V)


def _mla_sample(q_lat, q_pe, ckv_new, kpe_new, ckv_pool, kpe_pool, page_table, w_uv):
    db, t = q_lat.shape[:2]
    ckv_past = ckv_pool[page_table].reshape(db, -1, MLA_KV_LORA)
    kpe_past = kpe_pool[page_table].reshape(db, -1, MLA_D_ROPE)
    past = ckv_past.shape[1]
    s = jnp.concatenate([_mla_scores(q_lat, q_pe, ckv_past, kpe_past),
                         _mla_scores(q_lat, q_pe, ckv_new, kpe_new)], axis=-1)
    mask = jnp.concatenate([jnp.ones((t, past), dtype=bool), jnp.tril(jnp.ones((t, t), dtype=bool))], axis=1)
    p = _masked_softmax(s, mask).astype(ckv_new.dtype)
    o = (jnp.einsum('bhts,bsc->bthc', p[..., :past], ckv_past)
         + jnp.einsum('bhts,bsc->bthc', p[..., past:], ckv_new))
    return jnp.einsum('bthc,chd->bthd', o, w_uv).reshape(db, t, MLA_HEADS * MLA_D_V)


def _moba_prompt(q, k, v):
    bsz, s, nh, hd = q.shape
    nb = -(-s // MOBA_BLOCK)
    pad = nb * MOBA_BLOCK - s
    kb = jnp.pad(k, ((0, 0), (0, pad), (0, 0), (0, 0))).reshape(bsz, nb, MOBA_BLOCK, nh, hd)
    vb = jnp.pad(v, ((0, 0), (0, pad), (0, 0), (0, 0))).reshape(bsz, nb, MOBA_BLOCK, nh, hd)
    n_sel = min(MOBA_TOPK, nb - 1)
    kmean = jnp.mean(kb.astype(jnp.float32), axis=2)
    bb = jnp.arange(bsz)[:, None, None, None]
    hh = jnp.arange(nh)[None, :, None, None]

    def one(c):
        q0 = c * MOBA_Q_CHUNK
        qc = lax.dynamic_slice_in_dim(q, q0, MOBA_Q_CHUNK, axis=1)
        qpos = q0 + jnp.arange(MOBA_Q_CHUNK)
        j = q0 // MOBA_BLOCK
        k_own = lax.dynamic_index_in_dim(kb, j, axis=1, keepdims=False)
        v_own = lax.dynamic_index_in_dim(vb, j, axis=1, keepdims=False)
        kpos = j * MOBA_BLOCK + jnp.arange(MOBA_BLOCK)
        s_own = jnp.einsum('bthd,bshd->bhts', qc, k_own) * MOBA_SCALE
        m_own = kpos[None, :] <= qpos[:, None]
        if n_sel == 0:
            p = _masked_softmax(s_own, m_own).astype(v.dtype)
            return jnp.einsum('bhts,bshd->bthd', p, v_own)
        gate = jnp.einsum('bthd,bnhd->bhtn', qc.astype(jnp.float32), kmean)
        gate = jnp.where(jnp.arange(nb) < j, gate, -jnp.inf)
        _, idx = lax.top_k(gate, n_sel)
        k_sel = kb[bb, idx, :, hh, :]
        v_sel = vb[bb, idx, :, hh, :]
        s_sel = jnp.einsum('bthd,bhtkld->bhtkl', qc, k_sel).reshape(
            bsz, nh, MOBA_Q_CHUNK, n_sel * MOBA_BLOCK) * MOBA_SCALE
        m_sel = jnp.repeat(jnp.arange(n_sel) < j, MOBA_BLOCK)
        mask = jnp.concatenate([jnp.broadcast_to(m_sel[None, :], (MOBA_Q_CHUNK, n_sel * MOBA_BLOCK)), m_own], axis=1)
        p = _masked_softmax(jnp.concatenate([s_sel, s_own], axis=-1), mask).astype(v.dtype)
        p_sel = p[..., :n_sel * MOBA_BLOCK].reshape(bsz, nh, MOBA_Q_CHUNK, n_sel, MOBA_BLOCK)
        return (jnp.einsum('bhtkl,bhtkld->bthd', p_sel, v_sel)
                + jnp.einsum('bhts,bshd->bthd', p[..., n_sel * MOBA_BLOCK:], v_own))

    o = lax.map(one, jnp.arange(s // MOBA_Q_CHUNK))
    return jnp.moveaxis(o, 0, 1).reshape(bsz, s, nh * hd)


def _moba_sample(q, k_new, v_new, k_pool, v_pool, page_table):
    db, t, nh, hd = q.shape
    past = page_table.shape[1] * PAGE_SIZE
    nb_past = past // MOBA_BLOCK
    rem = past - nb_past * MOBA_BLOCK
    ppb = MOBA_BLOCK // PAGE_SIZE
    n_sel = min(MOBA_TOPK, nb_past)
    hh3 = jnp.arange(nh)[:, None, None]
    hh4 = jnp.arange(nh)[:, None, None, None]
    causal_new = jnp.tril(jnp.ones((t, t), dtype=bool))

    def one(args):
        qs, ks, vs, row = args
        k_past = k_pool[row].reshape(past, nh, hd)
        if rem > 0:
            k_own = jnp.concatenate([k_past[nb_past * MOBA_BLOCK:], ks], axis=0)
            v_tail = v_pool[row[nb_past * ppb:]].reshape(rem, nh, hd)
            v_own = jnp.concatenate([v_tail, vs], axis=0)
            m_own = jnp.concatenate([jnp.ones((t, rem), dtype=bool), causal_new], axis=1)
        else:
            k_own, v_own, m_own = ks, vs, causal_new
        s_own = jnp.einsum('thd,shd->hts', qs, k_own) * MOBA_SCALE
        if n_sel == 0:
            p = _masked_softmax(s_own, m_own).astype(vs.dtype)
            return jnp.einsum('hts,shd->thd', p, v_own)
        kb = k_past[:nb_past * MOBA_BLOCK].reshape(nb_past, MOBA_BLOCK, nh, hd)
        kmean = jnp.mean(kb.astype(jnp.float32), axis=1)
        gate = jnp.einsum('thd,nhd->htn', qs.astype(jnp.float32), kmean)
        _, idx = lax.top_k(gate, n_sel)
        k_sel = kb[idx, :, hh3, :].reshape(nh, t, n_sel * MOBA_BLOCK, hd)
        phys = row[idx[..., None] * ppb + jnp.arange(ppb)]
        v_sel = v_pool[phys, :, hh4, :].reshape(nh, t, n_sel * MOBA_BLOCK, hd)
        s_sel = jnp.einsum('thd,htsd->hts', qs, k_sel) * MOBA_SCALE
        mask = jnp.concatenate([jnp.ones((t, n_sel * MOBA_BLOCK), dtype=bool), m_own], axis=1)
        p = _masked_softmax(jnp.concatenate([s_sel, s_own], axis=-1), mask).astype(vs.dtype)
        return (jnp.einsum('hts,htsd->thd', p[..., :n_sel * MOBA_BLOCK], v_sel)
                + jnp.einsum('hts,shd->thd', p[..., n_sel * MOBA_BLOCK:], v_own))

    o = lax.map(one, (q, k_new, v_new, page_table))
    return o.reshape(db, t, nh * hd)


def _hier_moe(h, w_gr, b_gr, w_er, b_er, w_eg, w_eu, w_ed):
    shp = h.shape
    x = h.reshape(-1, D_MODEL)
    n = x.shape[0]
    g_prob = jax.nn.softmax((x @ w_gr).astype(jnp.float32) + b_gr.astype(jnp.float32), axis=-1)
    g_w, g_idx = lax.top_k(g_prob, 1)
    e_logit = ((x @ w_er).astype(jnp.float32) + b_er.astype(jnp.float32)).reshape(
        n, MOE_GROUPS, MOE_EXPERTS_PER_GROUP)
    e_logit = jnp.take_along_axis(e_logit, g_idx[:, :, None], axis=1)[:, 0]
    e_w, e_idx = lax.top_k(jax.nn.softmax(e_logit, axis=-1), MOE_TOPK)
    gate = g_w * e_w / jnp.sum(e_w, axis=-1, keepdims=True)
    expert = g_idx * MOE_EXPERTS_PER_GROUP + e_idx
    a = n * MOE_TOPK
    e_flat = expert.reshape(-1)
    order = jnp.argsort(e_flat)
    e_sorted = e_flat[order]
    tok_sorted = (order // MOE_TOPK).astype(jnp.int32)
    w_sorted = gate.reshape(-1)[order]
    counts = jnp.bincount(e_flat, length=MOE_EXPERTS)
    padded = (counts + MOE_ROW_BLOCK - 1) // MOE_ROW_BLOCK * MOE_ROW_BLOCK
    pad_end = jnp.cumsum(padded)
    pad_start = pad_end - padded
    start = jnp.cumsum(counts) - counts
    dest = pad_start[e_sorted] + jnp.arange(a) - start[e_sorted]
    n_blocks = -(-a // MOE_ROW_BLOCK) + MOE_EXPERTS
    rows = n_blocks * MOE_ROW_BLOCK
    row_tok = jnp.full((rows,), n, dtype=jnp.int32).at[dest].set(tok_sorted)
    row_w = jnp.zeros((rows,), dtype=jnp.float32).at[dest].set(w_sorted)
    blk_exp = jnp.minimum(jnp.searchsorted(pad_end, jnp.arange(n_blocks) * MOE_ROW_BLOCK, side='right'),
                          MOE_EXPERTS - 1)
    x_pad = jnp.concatenate([x, jnp.zeros((1, D_MODEL), dtype=x.dtype)], axis=0)
    xb = x_pad[row_tok].reshape(n_blocks, MOE_ROW_BLOCK, D_MODEL)

    def run(args):
        xi, e = args
        return (jax.nn.silu(xi @ w_eg[e]) * (xi @ w_eu[e])) @ w_ed[e]

    yb = lax.map(run, (xb, blk_exp)).reshape(rows, D_MODEL)
    y = jax.ops.segment_sum(yb * row_w[:, None].astype(yb.dtype), row_tok, num_segments=n + 1)[:n]
    return y.reshape(shp)


def _post(x, o_mix, p, w_out, g_ffn, w_gr, b_gr, w_er, b_er, w_eg, w_eu, w_ed, g_ple, w_ple_gate, w_ple_proj):
    x = x + o_mix @ w_out
    x = x + _hier_moe(_rmsnorm(x, g_ffn), w_gr, b_gr, w_er, b_er, w_eg, w_eu, w_ed)
    return x + (p @ w_ple_proj) * jax.nn.sigmoid(_rmsnorm(x, g_ple) @ w_ple_gate)


def setup_inputs(seed: int = 0) -> dict:
    key = jax.random.key(seed)
    ks = jax.random.split(key, 40)
    f32 = jnp.float32
    n_pages = PAST_LEN // PAGE_SIZE
    n_phys = (DEC_BATCH * n_pages * 5) // 4

    def nrm(k, shape, scale):
        return jax.random.normal(k, shape, f32) * scale

    def gain(k, shape):
        return 1.0 + 0.01 * jax.random.normal(k, shape, f32)

    page_table = jax.random.permutation(ks[8], n_phys)[:DEC_BATCH * n_pages].reshape(
        DEC_BATCH, n_pages).astype(jnp.int32)
    return {
        'x_prompt': nrm(ks[0], (BATCH, SEQ, D_MODEL), 1.0),
        'x_sample': nrm(ks[1], (DEC_BATCH, DEC_SEQ, D_MODEL), 1.0),
        'cache_ckv': nrm(ks[2], (DEPTH, n_phys, PAGE_SIZE, MLA_KV_LORA), 1.0),
        'cache_kpe': nrm(ks[3], (DEPTH, n_phys, PAGE_SIZE, MLA_D_ROPE), 1.0),
        'cache_k': nrm(ks[4], (DEPTH, n_phys, PAGE_SIZE, MOBA_HEADS, MOBA_HEAD_DIM), 1.0),
        'cache_v': nrm(ks[5], (DEPTH, n_phys, PAGE_SIZE, MOBA_HEADS, MOBA_HEAD_DIM), 1.0),
        'page_table': page_table,
        'p_prompt': nrm(ks[6], (DEPTH, BATCH, SEQ, PLE_DIM), 1.0),
        'p_sample': nrm(ks[7], (DEPTH, DEC_BATCH, DEC_SEQ, PLE_DIM), 1.0),
        'g_attn': gain(ks[9], (DEPTH, D_MODEL)),
        'w_in': nrm(ks[10], (DEPTH, D_MODEL, IN_COLS), D_MODEL ** -0.5),
        'g_q': gain(ks[11], (DEPTH, MLA_Q_LORA)),
        'w_uq': nrm(ks[12], (DEPTH, MLA_Q_LORA, MLA_HEADS, MLA_D_NOPE + MLA_D_ROPE), MLA_Q_LORA ** -0.5),
        'g_kv': gain(ks[13], (DEPTH, MLA_KV_LORA)),
        'w_uk': nrm(ks[14], (DEPTH, MLA_KV_LORA, MLA_HEADS, MLA_D_NOPE), MLA_KV_LORA ** -0.5),
        'w_uv': nrm(ks[15], (DEPTH, MLA_KV_LORA, MLA_HEADS, MLA_D_V), MLA_KV_LORA ** -0.5),
        'w_out': nrm(ks[16], (DEPTH, MIX_WIDTH, D_MODEL), MIX_WIDTH ** -0.5),
        'g_ffn': gain(ks[17], (DEPTH, D_MODEL)),
        'w_group_router': nrm(ks[18], (DEPTH, D_MODEL, MOE_GROUPS), D_MODEL ** -0.5),
        'b_group_router': nrm(ks[19], (DEPTH, MOE_GROUPS), 0.01),
        'w_expert_router': nrm(ks[20], (DEPTH, D_MODEL, MOE_EXPERTS), D_MODEL ** -0.5),
        'b_expert_router': nrm(ks[21], (DEPTH, MOE_EXPERTS), 0.01),
        'w_exp_gate': nrm(ks[22], (DEPTH, MOE_EXPERTS, D_MODEL, D_EXPERT), D_MODEL ** -0.5),
        'w_exp_up': nrm(ks[23], (DEPTH, MOE_EXPERTS, D_MODEL, D_EXPERT), D_MODEL ** -0.5),
        'w_exp_down': nrm(ks[24], (DEPTH, MOE_EXPERTS, D_EXPERT, D_MODEL), D_EXPERT ** -0.5),
        'g_ple': gain(ks[25], (DEPTH, D_MODEL)),
        'w_ple_gate': nrm(ks[26], (DEPTH, D_MODEL, D_MODEL), D_MODEL ** -0.5),
        'w_ple_proj': nrm(ks[27], (DEPTH, PLE_DIM, D_MODEL), PLE_DIM ** -0.5),
        'g_final': gain(ks[28], (D_MODEL,)),
    }


def reference(x_prompt, x_sample, cache_ckv, cache_kpe, cache_k, cache_v, page_table, p_prompt, p_sample,
              g_attn, w_in, g_q, w_uq, g_kv, w_uk, w_uv, w_out, g_ffn, w_group_router, b_group_router,
              w_expert_router, b_expert_router, w_exp_gate, w_exp_up, w_exp_down, g_ple, w_ple_gate,
              w_ple_proj, g_final):
    pos_p = jnp.arange(x_prompt.shape[1], dtype=jnp.int32)
    pos_s = PAST_LEN + jnp.arange(x_sample.shape[1], dtype=jnp.int32)
    xp, xs = x_prompt, x_sample
    ckv_p, kpe_p, k_p, v_p = [], [], [], []
    ckv_s, kpe_s, k_s, v_s = [], [], [], []
    for i in range(DEPTH):
        q_lat, q_pe, ckv, kpe, qb, kb, vb = _project(xp, pos_p, g_attn[i], w_in[i], g_q[i], w_uq[i], g_kv[i], w_uk[i])
        o_mix = jnp.concatenate([_mla_prompt(q_lat, q_pe, ckv, kpe, w_uv[i]), _moba_prompt(qb, kb, vb)], axis=-1)
        xp = _post(xp, o_mix, p_prompt[i], w_out[i], g_ffn[i], w_group_router[i], b_group_router[i],
                   w_expert_router[i], b_expert_router[i], w_exp_gate[i], w_exp_up[i], w_exp_down[i],
                   g_ple[i], w_ple_gate[i], w_ple_proj[i])
        ckv_p.append(ckv)
        kpe_p.append(kpe)
        k_p.append(kb)
        v_p.append(vb)
        q_lat, q_pe, ckv, kpe, qb, kb, vb = _project(xs, pos_s, g_attn[i], w_in[i], g_q[i], w_uq[i], g_kv[i], w_uk[i])
        o_mix = jnp.concatenate([
            _mla_sample(q_lat, q_pe, ckv, kpe, cache_ckv[i], cache_kpe[i], page_table, w_uv[i]),
            _moba_sample(qb, kb, vb, cache_k[i], cache_v[i], page_table)], axis=-1)
        xs = _post(xs, o_mix, p_sample[i], w_out[i], g_ffn[i], w_group_router[i], b_group_router[i],
                   w_expert_router[i], b_expert_router[i], w_exp_gate[i], w_exp_up[i], w_exp_down[i],
                   g_ple[i], w_ple_gate[i], w_ple_proj[i])
        ckv_s.append(ckv)
        kpe_s.append(kpe)
        k_s.append(kb)
        v_s.append(vb)
    y_prompt = _rmsnorm(xp, g_final)
    y_sample = _rmsnorm(xs, g_final)
    return (y_prompt, y_sample,
            jnp.stack(ckv_p), jnp.stack(kpe_p), jnp.stack(k_p), jnp.stack(v_p),
            jnp.stack(ckv_s), jnp.stack(kpe_s), jnp.stack(k_s), jnp.stack(v_s))
```

```python
import functools

import jax
import jax.numpy as jnp
import numpy as np
from jax import lax
from jax.experimental import pallas as pl
from jax.experimental.pallas import tpu as pltpu

D_MODEL = 1024
PAGE_SIZE = 128
MLA_HEADS = 8
MLA_Q_LORA = 256
MLA_KV_LORA = 128
MLA_D_NOPE = 64
MLA_D_ROPE = 32
MLA_D_V = 64
MLA_SCALE = (MLA_D_NOPE + MLA_D_ROPE) ** -0.5
MLA_QK = MLA_KV_LORA + MLA_D_ROPE
MOBA_HEADS = 8
MOBA_HEAD_DIM = 64
MOBA_WIDTH = MOBA_HEADS * MOBA_HEAD_DIM
MOBA_BLOCK = 256
MOBA_TOPK = 3
MOBA_SCALE = MOBA_HEAD_DIM ** -0.5
MOE_GROUPS = 4
MOE_EXPERTS_PER_GROUP = 8
MOE_EXPERTS = MOE_GROUPS * MOE_EXPERTS_PER_GROUP
D_EXPERT = 512
PLE_DIM = 256
ROPE_THETA = 10000.0
NORM_EPS = 1e-6

LANES = 128
NEG_BIG = -1e9

ROW_TILE = 256
MLA_TQ = 128
MLA_TK = 512
MOE_ROWS = 256
VMEM_LIMIT = 48 * 1024 * 1024

_bf16 = jnp.bfloat16
_f32 = jnp.float32


def _dot(a, b):
    return jnp.dot(a, b, preferred_element_type=_f32)


def _dot_nt(a, b):
    return lax.dot_general(a, b, (((1,), (1,)), ((), ())), preferred_element_type=_f32)


def _rms(x, g):
    return x * lax.rsqrt(jnp.mean(x * x, axis=-1, keepdims=True) + NORM_EPS) * g


def _rope_apply(x, cos, sin_lo, sin_hi, half):
    w = x.shape[-1]
    return x * cos + pltpu.roll(x, w - half, 1) * sin_lo + pltpu.roll(x, half, 1) * sin_hi


def _proj_kernel(x_ref, g_attn_ref, w_in_ref, g_q_ref, w_uqn_ref, w_uqr_ref, g_kv_ref, w_ukp_ref,
                 cb_ref, slb_ref, shb_ref, ca_ref, sla_ref, sha_ref,
                 ckv_ref, kpe_ref, kb_ref, vb_ref, qcat_ref, *rest, sample, tiles_per_seq):
    tm = x_ref.shape[0]
    h = _rms(x_ref[...], g_attn_ref[...])
    z = _dot(h.astype(_bf16), w_in_ref[...])
    c_q = z[:, 0:256]
    c_kv = z[:, 256:384]
    q_b = z[:, 384:896]
    k_b = z[:, 896:1408]
    v_b = z[:, 1408:1920]
    k_pe = z[:, 1920:2048]

    cqn = _rms(c_q, g_q_ref[...]).astype(_bf16)
    q_nope = _dot(cqn, w_uqn_ref[...])
    q_rope = _dot(cqn, w_uqr_ref[...])
    ca, sla, sha = ca_ref[...], sla_ref[...], sha_ref[...]
    q_pe = _rope_apply(q_rope, ca, sla, sha, MLA_D_ROPE // 2) * MLA_SCALE
    ckv = _rms(c_kv, g_kv_ref[...])
    kpe = _rope_apply(k_pe, ca[:, :LANES], sla[:, :LANES], sha[:, :LANES], MLA_D_ROPE // 2)[:, :MLA_D_ROPE]
    ckv_ref[...] = ckv
    kpe_ref[...] = kpe

    cb, slb, shb = cb_ref[...], slb_ref[...], shb_ref[...]
    q_b = _rope_apply(q_b, cb, slb, shb, MOBA_HEAD_DIM // 2) * MOBA_SCALE
    k_b = _rope_apply(k_b, cb, slb, shb, MOBA_HEAD_DIM // 2)
    kb_ref[...] = k_b
    vb_ref[...] = v_b

    for p in range(MLA_HEADS // 2):
        qn = q_nope[:, p * LANES:(p + 1) * LANES].astype(_bf16)
        ql = _dot(qn, w_ukp_ref[p]) * MLA_SCALE
        for u in range(2):
            hd = 2 * p + u
            qcat_ref[hd, :, 0:MLA_KV_LORA] = ql[:, u * LANES:(u + 1) * LANES].astype(_bf16)
            qcat_ref[hd, :, MLA_KV_LORA:MLA_QK] = q_pe[:, hd * MLA_D_ROPE:(hd + 1) * MLA_D_ROPE].astype(_bf16)

    if sample:
        (qb_ref,) = rest
        qb_ref[...] = q_b
        return

    kcat_ref, qbh_ref, kaug_ref, vbh_ref, kmean_ref = rest
    kcat_ref[:, 0:MLA_KV_LORA] = ckv.astype(_bf16)
    kcat_ref[:, MLA_KV_LORA:MLA_QK] = kpe.astype(_bf16)
    kmean_ref[0] = jnp.sum(k_b, axis=0, keepdims=True) * (1.0 / MOBA_BLOCK)
    blk = pl.program_id(0) % tiles_per_seq
    lane = lax.broadcasted_iota(jnp.int32, (tm, LANES), 1)
    onehot = jnp.where(lane == MOBA_HEAD_DIM + blk, 1.0, 0.0)
    for p in range(MOBA_HEADS // 2):
        sl = slice(p * LANES, (p + 1) * LANES)
        qp, kp, vp = q_b[:, sl], k_b[:, sl], v_b[:, sl]
        kp_sw = pltpu.roll(kp, MOBA_HEAD_DIM, 1)
        for u, ksrc in ((0, kp), (1, kp_sw)):
            hd = 2 * p + u
            kaug_ref[hd] = jnp.where(lane < MOBA_HEAD_DIM, ksrc, onehot).astype(_bf16)
            qbh_ref[hd] = qp[:, u * MOBA_HEAD_DIM:(u + 1) * MOBA_HEAD_DIM].astype(_bf16)
            vbh_ref[hd] = vp[:, u * MOBA_HEAD_DIM:(u + 1) * MOBA_HEAD_DIM].astype(_bf16)


def _rope_tables(pos, d, reps):
    half = d // 2
    inv = jnp.power(ROPE_THETA, -jnp.arange(half, dtype=_f32) * (2.0 / d))
    ang = pos.astype(_f32)[:, None] * inv[None, :]
    cos, sin, zero = jnp.cos(ang), jnp.sin(ang), jnp.zeros_like(ang)
    tile = lambda a, b: jnp.tile(jnp.concatenate([a, b], axis=1), (1, reps))
    return tile(cos, cos), tile(-sin, zero), tile(zero, sin)


def _project(x, pos, wts, *, sample):
    n = x.shape[0]
    tm = ROW_TILE
    assert n % tm == 0 and pos.shape[0] % tm == 0
    tiles_per_seq = pos.shape[0] // tm
    tabs_b = _rope_tables(pos, MOBA_HEAD_DIM, MOBA_HEADS)
    tabs_a = _rope_tables(pos, MLA_D_ROPE, MLA_HEADS)
    row = lambda w: pl.BlockSpec((tm, w), lambda i: (i, 0))
    full = lambda a: pl.BlockSpec(a.shape, lambda i: (0,) * a.ndim)
    tab = lambda w: pl.BlockSpec((tm, w), lambda i: (i % tiles_per_seq, 0))
    heads = lambda w: pl.BlockSpec((MLA_HEADS, tm, w), lambda i: (0, i, 0))
    out_shape = [jax.ShapeDtypeStruct((n, MLA_KV_LORA), _f32), jax.ShapeDtypeStruct((n, MLA_D_ROPE), _f32),
                 jax.ShapeDtypeStruct((n, MOBA_WIDTH), _f32), jax.ShapeDtypeStruct((n, MOBA_WIDTH), _f32),
                 jax.ShapeDtypeStruct((MLA_HEADS, n, MLA_QK), _bf16)]
    out_specs = [row(MLA_KV_LORA), row(MLA_D_ROPE), row(MOBA_WIDTH), row(MOBA_WIDTH), heads(MLA_QK)]
    if sample:
        out_shape += [jax.ShapeDtypeStruct((n, MOBA_WIDTH), _f32)]
        out_specs += [row(MOBA_WIDTH)]
    else:
        out_shape += [jax.ShapeDtypeStruct((n, MLA_QK), _bf16),
                      jax.ShapeDtypeStruct((MOBA_HEADS, n, MOBA_HEAD_DIM), _bf16),
                      jax.ShapeDtypeStruct((MOBA_HEADS, n, LANES), _bf16),
                      jax.ShapeDtypeStruct((MOBA_HEADS, n, MOBA_HEAD_DIM), _bf16),
                      jax.ShapeDtypeStruct((n // tm, 1, MOBA_WIDTH), _f32)]
        out_specs += [row(MLA_QK), heads(MOBA_HEAD_DIM), heads(LANES), heads(MOBA_HEAD_DIM),
                      pl.BlockSpec((1, 1, MOBA_WIDTH), lambda i: (i, 0, 0))]
    w = wts
    ins = [x, w['g_attn'], w['w_in'], w['g_q'], w['w_uqn'], w['w_uqr'], w['g_kv'], w['w_ukp'], *tabs_b, *tabs_a]
    in_specs = [row(D_MODEL)] + [full(a) for a in ins[1:8]] + [tab(MOBA_WIDTH)] * 3 + [tab(MLA_HEADS * MLA_D_ROPE)] * 3
    return pl.pallas_call(
        functools.partial(_proj_kernel, sample=sample, tiles_per_seq=tiles_per_seq),
        grid=(n // tm,), in_specs=in_specs, out_specs=out_specs, out_shape=out_shape,
        compiler_params=pltpu.CompilerParams(dimension_semantics=("parallel",), vmem_limit_bytes=VMEM_LIMIT),
    )(*ins)


def _mla_prompt_kernel(qi_ref, ki_ref, q_ref, k_ref, wuv_ref, o_ref, m_sc, l_sc, acc_sc, *, tq, tk):
    s_id = pl.program_id(1)
    qi, ki = qi_ref[s_id], ki_ref[s_id]
    rows = MLA_HEADS * tq
    last_k = (qi * tq) // tk

    @pl.when(ki == 0)
    def _():
        m_sc[...] = jnp.full(m_sc.shape, -jnp.inf, _f32)
        l_sc[...] = jnp.zeros(l_sc.shape, _f32)
        acc_sc[...] = jnp.zeros(acc_sc.shape, _f32)

    q = q_ref[...].reshape(rows, MLA_QK)
    k = k_ref[0]
    s = _dot_nt(q, k)

    def update(s):
        m_prev = m_sc[...]
        m_new = jnp.maximum(m_prev, jnp.max(s, axis=-1, keepdims=True))
        alpha = jnp.exp(m_prev - m_new)
        p = jnp.exp(s - m_new)
        l_sc[...] = alpha * l_sc[...] + jnp.sum(p, axis=-1, keepdims=True)
        acc_sc[...] = alpha * acc_sc[...] + _dot(p.astype(_bf16), k[:, :MLA_KV_LORA])
        m_sc[...] = m_new

    @pl.when(ki < last_k)
    def _():
        update(s)

    @pl.when(ki == last_k)
    def _():
        qpos = qi * tq + lax.broadcasted_iota(jnp.int32, (rows, tk), 0) % tq
        kpos = ki * tk + lax.broadcasted_iota(jnp.int32, (rows, tk), 1)
        update(jnp.where(kpos <= qpos, s, -jnp.inf))
        o = acc_sc[...] / l_sc[...]
        for p in range(MLA_HEADS // 2):
            pair = jnp.concatenate([o[(2 * p) * tq:(2 * p + 1) * tq], o[(2 * p + 1) * tq:(2 * p + 2) * tq]], axis=1)
            o_ref[0, :, p * LANES:(p + 1) * LANES] = _dot(pair.astype(_bf16), wuv_ref[p]).astype(o_ref.dtype)


def _mla_prompt(qcat, kcat, w_uvp, bsz, seq):
    tq, tk = min(MLA_TQ, seq), min(MLA_TK, seq)
    assert seq % tq == 0 and seq % tk == 0 and tk % tq == 0
    nq = seq // tq
    pairs = [(q, k) for q in range(nq) for k in range((q * tq) // tk + 1)]
    qi = jnp.asarray([p[0] for p in pairs], jnp.int32)
    ki = jnp.asarray([p[1] for p in pairs], jnp.int32)
    rows = MLA_HEADS * tq
    grid_spec = pltpu.PrefetchScalarGridSpec(
        num_scalar_prefetch=2, grid=(bsz, len(pairs)),
        in_specs=[pl.BlockSpec((MLA_HEADS, tq, MLA_QK), lambda b, s, qi, ki: (0, b * nq + qi[s], 0)),
                  pl.BlockSpec((1, tk, MLA_QK), lambda b, s, qi, ki: (b, ki[s], 0)),
                  pl.BlockSpec(w_uvp.shape, lambda b, s, qi, ki: (0, 0, 0))],
        out_specs=pl.BlockSpec((1, tq, MLA_HEADS * MLA_D_V), lambda b, s, qi, ki: (b, qi[s], 0)),
        scratch_shapes=[pltpu.VMEM((rows, 1), _f32), pltpu.VMEM((rows, 1), _f32),
                        pltpu.VMEM((rows, MLA_KV_LORA), _f32)])
    return pl.pallas_call(
        functools.partial(_mla_prompt_kernel, tq=tq, tk=tk), grid_spec=grid_spec,
        out_shape=jax.ShapeDtypeStruct((bsz, seq, MLA_HEADS * MLA_D_V), _bf16),
        compiler_params=pltpu.CompilerParams(dimension_semantics=("parallel", "arbitrary"),
                                             vmem_limit_bytes=VMEM_LIMIT),
    )(qi, ki, qcat, kcat.reshape(bsz, seq, MLA_QK), w_uvp)


def _top_blocks(gate, lane, n_valid, first_lane):
    g = jnp.where((lane >= first_lane) & (lane < first_lane + n_valid), gate, -jnp.inf)
    chosen = jnp.zeros(gate.shape, jnp.bool_)
    for _ in range(MOBA_TOPK):
        mx = jnp.max(g, axis=-1, keepdims=True)
        idx = jnp.min(jnp.where(g == mx, lane, jnp.int32(2 ** 30)), axis=-1, keepdims=True)
        pick = (lane == idx) & (mx > -jnp.inf)
        chosen = chosen | pick
        g = jnp.where(pick, -jnp.inf, g)
    return chosen


def _moba_prompt_kernel(qi_ref, kk_ref, q_ref, kaug_ref, v_ref, kmean_ref, o_ref, qaug_sc, m_sc, l_sc, acc_sc):
    s_id = pl.program_id(2)
    j, kk = qi_ref[s_id], kk_ref[s_id]
    t = MOBA_BLOCK
    lane = lax.broadcasted_iota(jnp.int32, (t, LANES), 1)

    for u in range(2):
        @pl.when(kk == 0)
        def _():
            q = q_ref[u]
            gate = _dot_nt(q, kmean_ref[0, u])
            chosen = _top_blocks(gate, lane, j, MOBA_HEAD_DIM) | (lane == MOBA_HEAD_DIM + j)
            bias = jnp.where(chosen, 0.0, NEG_BIG)
            q_wide = jnp.concatenate([q.astype(_f32), jnp.zeros((t, MOBA_HEAD_DIM), _f32)], axis=1)
            qaug_sc[u] = jnp.where(lane < MOBA_HEAD_DIM, q_wide, bias).astype(_bf16)

        s = _dot_nt(qaug_sc[u], kaug_ref[u])
        v = v_ref[u]

        @pl.when(kk == 0)
        def _():
            rowi = lax.broadcasted_iota(jnp.int32, (t, t), 0)
            coli = lax.broadcasted_iota(jnp.int32, (t, t), 1)
            sm = jnp.where(coli <= rowi, s, -jnp.inf)
            m = jnp.max(sm, axis=-1, keepdims=True)
            p = jnp.exp(sm - m)
            m_sc[u] = m
            l_sc[u] = jnp.sum(p, axis=-1, keepdims=True)
            acc_sc[u] = _dot(p.astype(_bf16), v)

        @pl.when(kk > 0)
        def _():
            m_prev = m_sc[u]
            m_new = jnp.maximum(m_prev, jnp.max(s, axis=-1, keepdims=True))
            alpha = jnp.exp(m_prev - m_new)
            p = jnp.exp(s - m_new)
            l_sc[u] = alpha * l_sc[u] + jnp.sum(p, axis=-1, keepdims=True)
            acc_sc[u] = alpha * acc_sc[u] + _dot(p.astype(_bf16), v)
            m_sc[u] = m_new

    @pl.when(kk == j)
    def _():
        o = [(acc_sc[u] / l_sc[u]) for u in range(2)]
        o_ref[0, 0] = jnp.concatenate(o, axis=1).astype(o_ref.dtype)


def _moba_prompt(qbh, kaug, vbh, kmean, bsz, seq):
    t = MOBA_BLOCK
    assert seq % t == 0
    nb = seq // t
    assert nb <= LANES - MOBA_HEAD_DIM
    hp = MOBA_HEADS // 2
    km = kmean.reshape(bsz, nb, MOBA_HEADS, MOBA_HEAD_DIM).transpose(0, 2, 1, 3)
    km = jnp.pad(km, ((0, 0), (0, 0), (MOBA_HEAD_DIM, LANES - MOBA_HEAD_DIM - nb), (0, 0))).astype(_bf16)
    pairs = [(q, k) for q in range(nb) for k in range(q + 1)]
    qi = jnp.asarray([p[0] for p in pairs], jnp.int32)
    kk = jnp.asarray([p[1] for p in pairs], jnp.int32)
    qmap = lambda b, p, s, qi, kk: (p, b * nb + qi[s], 0)
    kmap = lambda b, p, s, qi, kk: (p, b * nb + qi[s] - kk[s], 0)
    grid_spec = pltpu.PrefetchScalarGridSpec(
        num_scalar_prefetch=2, grid=(bsz, hp, len(pairs)),
        in_specs=[pl.BlockSpec((2, t, MOBA_HEAD_DIM), qmap),
                  pl.BlockSpec((2, t, LANES), kmap),
                  pl.BlockSpec((2, t, MOBA_HEAD_DIM), kmap),
                  pl.BlockSpec((1, 2, LANES, MOBA_HEAD_DIM), lambda b, p, s, qi, kk: (b, p, 0, 0))],
        out_specs=pl.BlockSpec((1, 1, t, LANES), lambda b, p, s, qi, kk: (p, b, qi[s], 0)),
        scratch_shapes=[pltpu.VMEM((2, t, LANES), _bf16), pltpu.VMEM((2, t, 1), _f32),
                        pltpu.VMEM((2, t, 1), _f32), pltpu.VMEM((2, t, MOBA_HEAD_DIM), _f32)])
    return pl.pallas_call(
        _moba_prompt_kernel, grid_spec=grid_spec,
        out_shape=jax.ShapeDtypeStruct((hp, bsz, seq, LANES), _bf16),
        compiler_params=pltpu.CompilerParams(dimension_semantics=("parallel", "parallel", "arbitrary"),
                                             vmem_limit_bytes=VMEM_LIMIT),
    )(qi, kk, qbh, kaug, vbh, km)


def _prep_attn_weights(g_attn, w_in, g_q, w_uq, g_kv, w_uk, w_uv):
    c0, c1, c2 = MLA_Q_LORA, MLA_Q_LORA + MLA_KV_LORA, MLA_Q_LORA + MLA_KV_LORA + MLA_D_ROPE
    w_in_r = jnp.concatenate([w_in[:, :c1], w_in[:, c2:], w_in[:, c1:c2],
                              jnp.zeros((D_MODEL, LANES - MLA_D_ROPE), w_in.dtype)], axis=1)
    w_uqn = w_uq[:, :, :MLA_D_NOPE].reshape(MLA_Q_LORA, MLA_HEADS * MLA_D_NOPE)
    w_uqr = w_uq[:, :, MLA_D_NOPE:].reshape(MLA_Q_LORA, MLA_HEADS * MLA_D_ROPE)
    ukt = jnp.transpose(w_uk, (1, 2, 0))
    zk = jnp.zeros_like(ukt[0])
    w_ukp = jnp.stack([jnp.concatenate([jnp.concatenate([ukt[2 * p], zk], axis=1),
                                        jnp.concatenate([zk, ukt[2 * p + 1]], axis=1)], axis=0)
                       for p in range(MLA_HEADS // 2)])
    uv = jnp.transpose(w_uv, (1, 0, 2))
    zv = jnp.zeros_like(uv[0])
    w_uvp = jnp.stack([jnp.concatenate([jnp.concatenate([uv[2 * p], zv], axis=1),
                                        jnp.concatenate([zv, uv[2 * p + 1]], axis=1)], axis=0)
                       for p in range(MLA_HEADS // 2)])
    return {'g_attn': g_attn.reshape(1, -1), 'w_in': w_in_r.astype(_bf16), 'g_q': g_q.reshape(1, -1),
            'w_uqn': w_uqn.astype(_bf16), 'w_uqr': w_uqr.astype(_bf16), 'g_kv': g_kv.reshape(1, -1),
            'w_ukp': w_ukp.astype(_bf16), 'w_uvp': w_uvp.astype(_bf16)}


def _prep_post_weights(w_out, g_ffn, w_gr, b_gr, w_er, b_er, g_ple, w_ple_gate, w_ple_proj, g_final):
    pad = LANES - MOE_EXPERTS - MOE_GROUPS
    w_r = jnp.concatenate([w_er, w_gr, jnp.zeros((D_MODEL, pad), w_er.dtype)], axis=1)
    b_r = jnp.concatenate([b_er, b_gr, jnp.zeros((pad,), b_er.dtype)]).reshape(1, LANES)
    w_r_hi = w_r.astype(_bf16)
    w_r_lo = (w_r - w_r_hi.astype(_f32)).astype(_bf16)
    return {'w_out': w_out.astype(_bf16), 'g_ffn': g_ffn.reshape(1, -1), 'w_r_hi': w_r_hi, 'w_r_lo': w_r_lo,
            'b_r': b_r, 'g_ple': g_ple.reshape(1, -1), 'w_ple_gate': w_ple_gate.astype(_bf16),
            'w_ple_proj': w_ple_proj.astype(_bf16), 'g_final': g_final.reshape(1, -1)}


def _route_kernel(x_ref, omla_ref, omoba_ref, w_out_ref, g_ffn_ref, wr_hi_ref, wr_lo_ref, br_ref,
                  x1_ref, h2_ref, ri_ref, rw_ref, cnt_ref, cnt_sc):
    tm = x_ref.shape[0]

    @pl.when(pl.program_id(0) == 0)
    def _():
        cnt_sc[...] = jnp.zeros(cnt_sc.shape, _f32)

    om = jnp.concatenate([omla_ref[...]] + [omoba_ref[p] for p in range(MOBA_HEADS // 2)], axis=1)
    x1 = x_ref[...] + _dot(om, w_out_ref[...])
    x1_ref[...] = x1
    h2 = _rms(x1, g_ffn_ref[...])
    h2_ref[...] = h2
    hi = h2.astype(_bf16)
    lo = (h2 - hi.astype(_f32)).astype(_bf16)
    logits = _dot(hi, wr_hi_ref[...]) + _dot(lo, wr_hi_ref[...]) + _dot(hi, wr_lo_ref[...]) + br_ref[...]

    lane = lax.broadcasted_iota(jnp.int32, (tm, LANES), 1)
    big = jnp.int32(2 ** 30)
    first = lambda mask: jnp.min(jnp.where(mask, lane, big), axis=-1, keepdims=True)
    gl = jnp.where((lane >= MOE_EXPERTS) & (lane < MOE_EXPERTS + MOE_GROUPS), logits, -jnp.inf)
    gmax = jnp.max(gl, axis=-1, keepdims=True)
    g_w = 1.0 / jnp.sum(jnp.exp(gl - gmax), axis=-1, keepdims=True)
    g_idx = first(gl == gmax) - MOE_EXPERTS
    e0 = g_idx * MOE_EXPERTS_PER_GROUP
    el = jnp.where((lane >= e0) & (lane < e0 + MOE_EXPERTS_PER_GROUP), logits, -jnp.inf)
    emax = jnp.max(el, axis=-1, keepdims=True)
    esum = jnp.sum(jnp.exp(el - emax), axis=-1, keepdims=True)
    i1 = first(el == emax)
    el2 = jnp.where(lane == i1, -jnp.inf, el)
    e2max = jnp.max(el2, axis=-1, keepdims=True)
    i2 = first(el2 == e2max)
    p1 = 1.0 / esum
    p2 = jnp.exp(e2max - emax) / esum
    w1 = g_w * p1 / (p1 + p2)
    w2 = g_w * p2 / (p1 + p2)
    oh1, oh2 = lane == i1, lane == i2
    both = jnp.where(oh1 | oh2, 1.0, 0.0)
    rowi = lax.broadcasted_iota(jnp.int32, (tm, tm), 0)
    coli = lax.broadcasted_iota(jnp.int32, (tm, tm), 1)
    tri = jnp.where(coli < rowi, 1.0, 0.0).astype(_bf16)
    before = cnt_sc[...] + _dot(tri, both.astype(_bf16))
    r1 = jnp.sum(jnp.where(oh1, before, 0.0), axis=-1, keepdims=True).astype(jnp.int32)
    r2 = jnp.sum(jnp.where(oh2, before, 0.0), axis=-1, keepdims=True).astype(jnp.int32)
    cnt = cnt_sc[...] + jnp.sum(both, axis=0, keepdims=True)
    cnt_sc[...] = cnt
    cnt_ref[...] = cnt
    l8 = lax.broadcasted_iota(jnp.int32, (tm, 8), 1)
    ri_ref[...] = jnp.where(l8 == 0, i1, jnp.where(l8 == 1, i2, jnp.where(l8 == 2, r1, jnp.where(l8 == 3, r2, 0))))
    rw_ref[...] = jnp.where(l8 == 0, w1, jnp.where(l8 == 1, w2, 0.0))


def _route(x, o_mla, o_moba, w):
    n = x.shape[0]
    tm = ROW_TILE
    row = lambda wd: pl.BlockSpec((tm, wd), lambda i: (i, 0))
    full = lambda a: pl.BlockSpec(a.shape, lambda i: (0,) * a.ndim)
    ws = [w['w_out'], w['g_ffn'], w['w_r_hi'], w['w_r_lo'], w['b_r']]
    return pl.pallas_call(
        _route_kernel, grid=(n // tm,),
        in_specs=[row(D_MODEL), row(MLA_HEADS * MLA_D_V),
                  pl.BlockSpec((MOBA_HEADS // 2, tm, LANES), lambda i: (0, i, 0))] + [full(a) for a in ws],
        out_specs=[row(D_MODEL), row(D_MODEL), row(8), row(8), pl.BlockSpec((1, LANES), lambda i: (0, 0))],
        out_shape=[jax.ShapeDtypeStruct((n, D_MODEL), _f32), jax.ShapeDtypeStruct((n, D_MODEL), _f32),
                   jax.ShapeDtypeStruct((n, 8), jnp.int32), jax.ShapeDtypeStruct((n, 8), _f32),
                   jax.ShapeDtypeStruct((1, LANES), _f32)],
        scratch_shapes=[pltpu.VMEM((1, LANES), _f32)],
        compiler_params=pltpu.CompilerParams(dimension_semantics=("arbitrary",), vmem_limit_bytes=VMEM_LIMIT),
    )(x, o_mla, o_moba, *ws)


def _scatter_kernel(dest_ref, h_ref, xs_in_ref, xs_ref, sem):
    del xs_in_ref
    tm = h_ref.shape[0]

    def copy(r, k):
        return pltpu.make_async_copy(h_ref.at[pl.ds(r, 1)], xs_ref.at[pl.ds(dest_ref[2 * r + k], 1)], sem)

    def start(r, c):
        copy(r, 0).start()
        copy(r, 1).start()
        return c

    def wait(r, c):
        copy(r, 0).wait()
        copy(r, 1).wait()
        return c

    lax.fori_loop(0, tm, start, 0, unroll=8)
    lax.fori_loop(0, tm, wait, 0, unroll=8)


def _scatter_rows(h2, dest, rows):
    n = h2.shape[0]
    tm = ROW_TILE
    return pl.pallas_call(
        _scatter_kernel, grid=(n // tm,),
        in_specs=[pl.BlockSpec((2 * tm,), lambda i: (i,), memory_space=pltpu.SMEM),
                  pl.BlockSpec((tm, D_MODEL), lambda i: (i, 0)),
                  pl.BlockSpec(memory_space=pl.ANY)],
        out_specs=pl.BlockSpec(memory_space=pl.ANY),
        out_shape=jax.ShapeDtypeStruct((rows, D_MODEL), h2.dtype),
        scratch_shapes=[pltpu.SemaphoreType.DMA(())],
        input_output_aliases={2: 0},
        compiler_params=pltpu.CompilerParams(dimension_semantics=("arbitrary",), has_side_effects=True),
    )(dest.reshape(-1), h2, jnp.zeros((rows, D_MODEL), h2.dtype))


def _expert_kernel(be_ref, na_ref, xs_ref, wg_ref, wu_ref, wd_ref, yb_ref, wg_sc, wu_sc, wd_sc):
    b = pl.program_id(0)

    @pl.when(b < na_ref[0])
    def _():
        prev = be_ref[jnp.maximum(b - 1, 0)]

        @pl.when((b == 0) | (be_ref[b] != prev))
        def _():
            wg_sc[...] = wg_ref[0].astype(_bf16)
            wu_sc[...] = wu_ref[0].astype(_bf16)
            wd_sc[...] = wd_ref[0].astype(_bf16)

        xi = xs_ref[...].astype(_bf16)
        g = _dot(xi, wg_sc[...])
        u = _dot(xi, wu_sc[...])
        a = g * (1.0 / (1.0 + jnp.exp(-g))) * u
        yb_ref[...] = _dot(a.astype(_bf16), wd_sc[...])

    @pl.when(b >= na_ref[0])
    def _():
        yb_ref[...] = jnp.zeros(yb_ref.shape, yb_ref.dtype)


def _experts(xs, blk_exp, n_active, w_eg, w_eu, w_ed):
    rows = xs.shape[0]
    r = MOE_ROWS
    nblk = rows // r
    blk = lambda b, be, na: (jnp.minimum(b, na[0] - 1), 0)
    wmap = lambda b, be, na: (be[jnp.minimum(b, na[0] - 1)], 0, 0)
    grid_spec = pltpu.PrefetchScalarGridSpec(
        num_scalar_prefetch=2, grid=(nblk,),
        in_specs=[pl.BlockSpec((r, D_MODEL), blk),
                  pl.BlockSpec((1, D_MODEL, D_EXPERT), wmap), pl.BlockSpec((1, D_MODEL, D_EXPERT), wmap),
                  pl.BlockSpec((1, D_EXPERT, D_MODEL), wmap)],
        out_specs=pl.BlockSpec((r, D_MODEL), lambda b, be, na: (b, 0)),
        scratch_shapes=[pltpu.VMEM((D_MODEL, D_EXPERT), _bf16), pltpu.VMEM((D_MODEL, D_EXPERT), _bf16),
                        pltpu.VMEM((D_EXPERT, D_MODEL), _bf16)])
    return pl.pallas_call(
        _expert_kernel, grid_spec=grid_spec, out_shape=jax.ShapeDtypeStruct((rows, D_MODEL), _f32),
        compiler_params=pltpu.CompilerParams(dimension_semantics=("arbitrary",), vmem_limit_bytes=VMEM_LIMIT),
    )(blk_exp, n_active, xs, w_eg, w_eu, w_ed)


def _combine_kernel(dest_ref, yb_ref, x1_ref, rw_ref, p_ref, g_ple_ref, w_pg_ref, w_pp_ref, g_fin_ref,
                    y_ref, ybuf, sem, *, final_norm):
    tm = x1_ref.shape[0]

    def copy(r, k):
        return pltpu.make_async_copy(yb_ref.at[pl.ds(dest_ref[2 * r + k], 1)], ybuf.at[k, pl.ds(r, 1)], sem)

    def start(r, c):
        copy(r, 0).start()
        copy(r, 1).start()
        return c

    def wait(r, c):
        copy(r, 0).wait()
        copy(r, 1).wait()
        return c

    lax.fori_loop(0, tm, start, 0, unroll=8)
    ple = _dot(p_ref[...].astype(_bf16), w_pp_ref[...])
    lax.fori_loop(0, tm, wait, 0, unroll=8)
    rw = rw_ref[...]
    x2 = x1_ref[...] + (ybuf[0] * rw[:, 0:1] + ybuf[1] * rw[:, 1:2])
    gate = _dot(_rms(x2, g_ple_ref[...]).astype(_bf16), w_pg_ref[...])
    x3 = x2 + ple * (1.0 / (1.0 + jnp.exp(-gate)))
    y_ref[...] = _rms(x3, g_fin_ref[...]) if final_norm else x3


def _combine(yb, dest, x1, rw, p, w, final_norm):
    n = x1.shape[0]
    tm = ROW_TILE
    row = lambda wd: pl.BlockSpec((tm, wd), lambda i: (i, 0))
    full = lambda a: pl.BlockSpec(a.shape, lambda i: (0,) * a.ndim)
    ws = [w['g_ple'], w['w_ple_gate'], w['w_ple_proj'], w['g_final']]
    return pl.pallas_call(
        functools.partial(_combine_kernel, final_norm=final_norm), grid=(n // tm,),
        in_specs=[pl.BlockSpec((2 * tm,), lambda i: (i,), memory_space=pltpu.SMEM),
                  pl.BlockSpec(memory_space=pl.ANY), row(D_MODEL), row(8), row(PLE_DIM)] + [full(a) for a in ws],
        out_specs=row(D_MODEL),
        out_shape=jax.ShapeDtypeStruct((n, D_MODEL), _f32),
        scratch_shapes=[pltpu.VMEM((2, tm, D_MODEL), _f32), pltpu.SemaphoreType.DMA(())],
        compiler_params=pltpu.CompilerParams(dimension_semantics=("arbitrary",), vmem_limit_bytes=VMEM_LIMIT),
    )(dest.reshape(-1), yb, x1, rw, p, *ws)


def _post(x, o_mla, o_moba, p, w, w_eg, w_eu, w_ed, final_norm=True):
    n = x.shape[0]
    x1, h2, ri, rw, cnt = _route(x, o_mla, o_moba, w)
    counts = cnt[0, :MOE_EXPERTS].astype(jnp.int32)
    r = MOE_ROWS
    padded = (counts + r - 1) // r * r
    pad_end = jnp.cumsum(padded)
    pad_start = pad_end - padded
    dest = pad_start[ri[:, 0:2]] + ri[:, 2:4]
    nblk = -(-(2 * n) // r) + MOE_EXPERTS
    blk_exp = jnp.minimum(jnp.searchsorted(pad_end, jnp.arange(nblk, dtype=jnp.int32) * r, side='right'),
                          MOE_EXPERTS - 1).astype(jnp.int32)
    n_active = (pad_end[-1:] // r).astype(jnp.int32)
    xs = _scatter_rows(h2, dest, nblk * r)
    yb = _experts(xs, blk_exp, n_active, w_eg, w_eu, w_ed)
    return _combine(yb, dest, x1, rw, p, w, final_norm)


def _masked_softmax(s, mask):
    return jax.nn.softmax(jnp.where(mask, s.astype(_f32), -jnp.inf), axis=-1)


def _mla_sample_jnp(qcat, ckv_new, kpe_new, ckv_pool, kpe_pool, page_table, w_uv):
    db = page_table.shape[0]
    t = ckv_new.shape[0] // db
    q = qcat.astype(_f32).reshape(MLA_HEADS, db, t, MLA_QK)
    ckv_past = ckv_pool[page_table].reshape(db, -1, MLA_KV_LORA)
    kpe_past = kpe_pool[page_table].reshape(db, -1, MLA_D_ROPE)
    past = ckv_past.shape[1]
    k_past = jnp.concatenate([ckv_past, kpe_past], axis=-1)
    k_new = jnp.concatenate([ckv_new, kpe_new], axis=-1).reshape(db, t, MLA_QK)
    s = jnp.concatenate([jnp.einsum('hbtc,bsc->bhts', q, k_past), jnp.einsum('hbtc,bsc->bhts', q, k_new)], axis=-1)
    mask = jnp.concatenate([jnp.ones((t, past), dtype=bool), jnp.tril(jnp.ones((t, t), dtype=bool))], axis=1)
    p = _masked_softmax(s, mask)
    o = (jnp.einsum('bhts,bsc->bthc', p[..., :past], ckv_past)
         + jnp.einsum('bhts,bsc->bthc', p[..., past:], ckv_new.reshape(db, t, MLA_KV_LORA)))
    return jnp.einsum('bthc,chd->bthd', o, w_uv).reshape(db * t, MLA_HEADS * MLA_D_V).astype(_bf16)


def _moba_sample_jnp(qb, k_new, v_new, k_pool, v_pool, page_table):
    db = page_table.shape[0]
    nh, hd = MOBA_HEADS, MOBA_HEAD_DIM
    t = qb.shape[0] // db
    past = page_table.shape[1] * PAGE_SIZE
    assert past % MOBA_BLOCK == 0 and past // MOBA_BLOCK >= MOBA_TOPK
    nb_past = past // MOBA_BLOCK
    ppb = MOBA_BLOCK // PAGE_SIZE
    n_sel = MOBA_TOPK
    hh3 = jnp.arange(nh)[:, None, None]
    hh4 = jnp.arange(nh)[:, None, None, None]
    causal_new = jnp.tril(jnp.ones((t, t), dtype=bool))

    def one(args):
        qs, ks, vs, row = args
        k_past = k_pool[row].reshape(past, nh, hd)
        s_own = jnp.einsum('thd,shd->hts', qs, ks)
        kb = k_past.reshape(nb_past, MOBA_BLOCK, nh, hd)
        kmean = jnp.mean(kb, axis=1)
        gate = jnp.einsum('thd,nhd->htn', qs, kmean)
        _, idx = lax.top_k(gate, n_sel)
        k_sel = kb[idx, :, hh3, :].reshape(nh, t, n_sel * MOBA_BLOCK, hd)
        phys = row[idx[..., None] * ppb + jnp.arange(ppb)]
        v_sel = v_pool[phys, :, hh4, :].reshape(nh, t, n_sel * MOBA_BLOCK, hd)
        s_sel = jnp.einsum('thd,htsd->hts', qs, k_sel)
        mask = jnp.concatenate([jnp.ones((t, n_sel * MOBA_BLOCK), dtype=bool), causal_new], axis=1)
        p = _masked_softmax(jnp.concatenate([s_sel, s_own], axis=-1), mask)
        return (jnp.einsum('hts,htsd->thd', p[..., :n_sel * MOBA_BLOCK], v_sel)
                + jnp.einsum('hts,shd->thd', p[..., n_sel * MOBA_BLOCK:], vs))

    shp = (db, t, nh, hd)
    o = lax.map(one, (qb.reshape(shp), k_new.reshape(shp), v_new.reshape(shp), page_table))
    return o.reshape(db * t, nh * hd).astype(_bf16)


def kernel(x_prompt, x_sample, cache_ckv, cache_kpe, cache_k, cache_v, page_table, p_prompt, p_sample, g_attn, w_in,
           g_q, w_uq, g_kv, w_uk, w_uv, w_out, g_ffn, w_group_router, b_group_router, w_expert_router,
           b_expert_router, w_exp_gate, w_exp_up, w_exp_down, g_ple, w_ple_gate, w_ple_proj, g_final):
    bsz, seq, _ = x_prompt.shape
    db, dt, _ = x_sample.shape
    depth = g_attn.shape[0]
    n_p, n_s = bsz * seq, db * dt
    past = page_table.shape[1] * PAGE_SIZE
    assert ROW_TILE % dt == 0 and n_s % ROW_TILE == 0
    pos_p = jnp.arange(seq, dtype=jnp.int32)
    pos_s = past + jnp.arange(ROW_TILE, dtype=jnp.int32) % dt
    xp = x_prompt.reshape(n_p, D_MODEL)
    xs = x_sample.reshape(n_s, D_MODEL)
    outs_p, outs_s = [], []
    for i in range(depth):
        wa = _prep_attn_weights(g_attn[i], w_in[i], g_q[i], w_uq[i], g_kv[i], w_uk[i], w_uv[i])
        wp = _prep_post_weights(w_out[i], g_ffn[i], w_group_router[i], b_group_router[i], w_expert_router[i],
                                b_expert_router[i], g_ple[i], w_ple_gate[i], w_ple_proj[i], g_final)
        ckv, kpe, kb, vb, qcat, kcat, qbh, kaug, vbh, kmean = _project(xp, pos_p, wa, sample=False)
        o_mla = _mla_prompt(qcat, kcat, wa['w_uvp'], bsz, seq).reshape(n_p, MLA_HEADS * MLA_D_V)
        o_moba = _moba_prompt(qbh, kaug, vbh, kmean, bsz, seq).reshape(MOBA_HEADS // 2, n_p, LANES)
        outs_p.append((ckv, kpe, kb, vb))
        ckv_s, kpe_s, kb_s, vb_s, qcat_s, qb_s = _project(xs, pos_s, wa, sample=True)
        o_mla_s = _mla_sample_jnp(qcat_s, ckv_s, kpe_s, cache_ckv[i], cache_kpe[i], page_table, w_uv[i])
        o_moba_s = _moba_sample_jnp(qb_s, kb_s, vb_s, cache_k[i], cache_v[i], page_table)
        o_moba_s = o_moba_s.reshape(n_s, MOBA_HEADS // 2, LANES).transpose(1, 0, 2)
        outs_s.append((ckv_s, kpe_s, kb_s, vb_s))
        x_all = jnp.concatenate([xp, xs], axis=0)
        y_all = _post(x_all, jnp.concatenate([o_mla, o_mla_s], axis=0), jnp.concatenate([o_moba, o_moba_s], axis=1),
                      jnp.concatenate([p_prompt[i].reshape(n_p, PLE_DIM), p_sample[i].reshape(n_s, PLE_DIM)], axis=0),
                      wp, w_exp_gate[i], w_exp_up[i], w_exp_down[i], final_norm=(i == depth - 1))
        xp, xs = y_all[:n_p], y_all[n_p:]
    stack = lambda outs, j, shp: jnp.stack([o[j].reshape(shp) for o in outs])
    hs = (MOBA_HEADS, MOBA_HEAD_DIM)
    return (xp.reshape(bsz, seq, D_MODEL), xs.reshape(db, dt, D_MODEL),
            stack(outs_p, 0, (bsz, seq, MLA_KV_LORA)), stack(outs_p, 1, (bsz, seq, MLA_D_ROPE)),
            stack(outs_p, 2, (bsz, seq) + hs), stack(outs_p, 3, (bsz, seq) + hs),
            stack(outs_s, 0, (db, dt, MLA_KV_LORA)), stack(outs_s, 1, (db, dt, MLA_D_ROPE)),
            stack(outs_s, 2, (db, dt) + hs), stack(outs_s, 3, (db, dt) + hs))
```

```python
import functools

import jax
import jax.numpy as jnp
import numpy as np
from jax import lax
from jax.experimental import pallas as pl
from jax.experimental.pallas import tpu as pltpu

D_MODEL = 1024
PAGE_SIZE = 128
MLA_HEADS = 8
MLA_Q_LORA = 256
MLA_KV_LORA = 128
MLA_D_NOPE = 64
MLA_D_ROPE = 32
MLA_D_V = 64
MLA_SCALE = (MLA_D_NOPE + MLA_D_ROPE) ** -0.5
MLA_QK = MLA_KV_LORA + MLA_D_ROPE
MOBA_HEADS = 8
MOBA_HEAD_DIM = 64
MOBA_WIDTH = MOBA_HEADS * MOBA_HEAD_DIM
MOBA_BLOCK = 256
MOBA_TOPK = 3
MOBA_SCALE = MOBA_HEAD_DIM ** -0.5
MOE_GROUPS = 4
MOE_EXPERTS_PER_GROUP = 8
MOE_EXPERTS = MOE_GROUPS * MOE_EXPERTS_PER_GROUP
D_EXPERT = 512
PLE_DIM = 256
ROPE_THETA = 10000.0
NORM_EPS = 1e-6

LANES = 128
NEG_BIG = -1e9

ROW_TILE = 256
MLA_TQ = 128
MLA_TK = 512
MOBA_TK = 512
MOE_ROWS = 256
MLA_PAGES = 16
MOBA_PAGES = 16
MOBA_HEAD_GROUP = 4
VMEM_LIMIT = 48 * 1024 * 1024

_bf16 = jnp.bfloat16
_f32 = jnp.float32


def _dot(a, b):
    return jnp.dot(a, b, preferred_element_type=_f32)


def _dot_nt(a, b):
    return lax.dot_general(a, b, (((1,), (1,)), ((), ())), preferred_element_type=_f32)


def _rms(x, g):
    return x * lax.rsqrt(jnp.mean(x * x, axis=-1, keepdims=True) + NORM_EPS) * g


def _rope_apply(x, cos, sin_lo, sin_hi, half):
    w = x.shape[-1]
    return x * cos + pltpu.roll(x, w - half, 1) * sin_lo + pltpu.roll(x, half, 1) * sin_hi


def _proj_kernel(x_ref, g_attn_ref, w_in_ref, g_q_ref, w_uqn_ref, w_uqr_ref, g_kv_ref, w_ukp_ref,
                 cb_ref, slb_ref, shb_ref, ca_ref, sla_ref, sha_ref,
                 ckv_ref, kpe_ref, kb_ref, vb_ref, qcat_ref, *rest, sample, tiles_per_seq):
    tm = x_ref.shape[0]
    h = _rms(x_ref[...], g_attn_ref[...])
    z = _dot(h.astype(_bf16), w_in_ref[...])
    c_q = z[:, 0:256]
    c_kv = z[:, 256:384]
    q_b = z[:, 384:896]
    k_b = z[:, 896:1408]
    v_b = z[:, 1408:1920]
    k_pe = z[:, 1920:2048]

    cqn = _rms(c_q, g_q_ref[...]).astype(_bf16)
    q_nope = _dot(cqn, w_uqn_ref[...])
    q_rope = _dot(cqn, w_uqr_ref[...])
    ca, sla, sha = ca_ref[...], sla_ref[...], sha_ref[...]
    q_pe = _rope_apply(q_rope, ca, sla, sha, MLA_D_ROPE // 2) * MLA_SCALE
    ckv = _rms(c_kv, g_kv_ref[...])
    kpe = _rope_apply(k_pe, ca[:, :LANES], sla[:, :LANES], sha[:, :LANES], MLA_D_ROPE // 2)[:, :MLA_D_ROPE]
    ckv_ref[...] = ckv
    kpe_ref[...] = kpe

    cb, slb, shb = cb_ref[...], slb_ref[...], shb_ref[...]
    q_b = _rope_apply(q_b, cb, slb, shb, MOBA_HEAD_DIM // 2) * MOBA_SCALE
    k_b = _rope_apply(k_b, cb, slb, shb, MOBA_HEAD_DIM // 2)
    kb_ref[...] = k_b
    vb_ref[...] = v_b

    for p in range(MLA_HEADS // 2):
        qn = q_nope[:, p * LANES:(p + 1) * LANES].astype(_bf16)
        ql = _dot(qn, w_ukp_ref[p]) * MLA_SCALE
        for u in range(2):
            hd = 2 * p + u
            qcat_ref[hd, :, 0:MLA_KV_LORA] = ql[:, u * LANES:(u + 1) * LANES].astype(_bf16)
            qcat_ref[hd, :, MLA_KV_LORA:MLA_QK] = q_pe[:, hd * MLA_D_ROPE:(hd + 1) * MLA_D_ROPE].astype(_bf16)

    if sample:
        (qb_ref,) = rest
        qb_ref[...] = q_b
        return

    kcat_ref, qbh_ref, kaug_ref, vbh_ref, kmean_ref = rest
    kcat_ref[:, 0:MLA_KV_LORA] = ckv.astype(_bf16)
    kcat_ref[:, MLA_KV_LORA:MLA_QK] = kpe.astype(_bf16)
    kmean_ref[0] = jnp.sum(k_b, axis=0, keepdims=True) * (1.0 / MOBA_BLOCK)
    blk = pl.program_id(0) % tiles_per_seq
    lane = lax.broadcasted_iota(jnp.int32, (tm, LANES), 1)
    onehot = jnp.where(lane == MOBA_HEAD_DIM + blk, 1.0, 0.0)
    for p in range(MOBA_HEADS // 2):
        sl = slice(p * LANES, (p + 1) * LANES)
        qp, kp, vp = q_b[:, sl], k_b[:, sl], v_b[:, sl]
        kp_sw = pltpu.roll(kp, MOBA_HEAD_DIM, 1)
        for u, ksrc in ((0, kp), (1, kp_sw)):
            hd = 2 * p + u
            kaug_ref[hd] = jnp.where(lane < MOBA_HEAD_DIM, ksrc, onehot).astype(_bf16)
            qbh_ref[hd] = qp[:, u * MOBA_HEAD_DIM:(u + 1) * MOBA_HEAD_DIM].astype(_bf16)
            vbh_ref[hd] = vp[:, u * MOBA_HEAD_DIM:(u + 1) * MOBA_HEAD_DIM].astype(_bf16)


def _rope_tables(pos, d, reps):
    half = d // 2
    inv = jnp.power(ROPE_THETA, -jnp.arange(half, dtype=_f32) * (2.0 / d))
    ang = pos.astype(_f32)[:, None] * inv[None, :]
    cos, sin, zero = jnp.cos(ang), jnp.sin(ang), jnp.zeros_like(ang)
    tile = lambda a, b: jnp.tile(jnp.concatenate([a, b], axis=1), (1, reps))
    return tile(cos, cos), tile(-sin, zero), tile(zero, sin)


def _project(x, pos, wts, *, sample):
    n = x.shape[0]
    tm = ROW_TILE
    assert n % tm == 0 and pos.shape[0] % tm == 0
    tiles_per_seq = pos.shape[0] // tm
    tabs_b = _rope_tables(pos, MOBA_HEAD_DIM, MOBA_HEADS)
    tabs_a = _rope_tables(pos, MLA_D_ROPE, MLA_HEADS)
    row = lambda w: pl.BlockSpec((tm, w), lambda i: (i, 0))
    full = lambda a: pl.BlockSpec(a.shape, lambda i: (0,) * a.ndim)
    tab = lambda w: pl.BlockSpec((tm, w), lambda i: (i % tiles_per_seq, 0))
    heads = lambda w: pl.BlockSpec((MLA_HEADS, tm, w), lambda i: (0, i, 0))
    out_shape = [jax.ShapeDtypeStruct((n, MLA_KV_LORA), _f32), jax.ShapeDtypeStruct((n, MLA_D_ROPE), _f32),
                 jax.ShapeDtypeStruct((n, MOBA_WIDTH), _f32), jax.ShapeDtypeStruct((n, MOBA_WIDTH), _f32),
                 jax.ShapeDtypeStruct((MLA_HEADS, n, MLA_QK), _bf16)]
    out_specs = [row(MLA_KV_LORA), row(MLA_D_ROPE), row(MOBA_WIDTH), row(MOBA_WIDTH), heads(MLA_QK)]
    if sample:
        out_shape += [jax.ShapeDtypeStruct((n, MOBA_WIDTH), _f32)]
        out_specs += [row(MOBA_WIDTH)]
    else:
        out_shape += [jax.ShapeDtypeStruct((n, MLA_QK), _bf16),
                      jax.ShapeDtypeStruct((MOBA_HEADS, n, MOBA_HEAD_DIM), _bf16),
                      jax.ShapeDtypeStruct((MOBA_HEADS, n, LANES), _bf16),
                      jax.ShapeDtypeStruct((MOBA_HEADS, n, MOBA_HEAD_DIM), _bf16),
                      jax.ShapeDtypeStruct((n // tm, 1, MOBA_WIDTH), _f32)]
        out_specs += [row(MLA_QK), heads(MOBA_HEAD_DIM), heads(LANES), heads(MOBA_HEAD_DIM),
                      pl.BlockSpec((1, 1, MOBA_WIDTH), lambda i: (i, 0, 0))]
    w = wts
    ins = [x, w['g_attn'], w['w_in'], w['g_q'], w['w_uqn'], w['w_uqr'], w['g_kv'], w['w_ukp'], *tabs_b, *tabs_a]
    in_specs = [row(D_MODEL)] + [full(a) for a in ins[1:8]] + [tab(MOBA_WIDTH)] * 3 + [tab(MLA_HEADS * MLA_D_ROPE)] * 3
    return pl.pallas_call(
        functools.partial(_proj_kernel, sample=sample, tiles_per_seq=tiles_per_seq),
        grid=(n // tm,), in_specs=in_specs, out_specs=out_specs, out_shape=out_shape,
        compiler_params=pltpu.CompilerParams(dimension_semantics=("parallel",), vmem_limit_bytes=VMEM_LIMIT),
    )(*ins)


def _mla_prompt_kernel(qi_ref, ki_ref, q_ref, k_ref, wuv_ref, o_ref, m_sc, l_sc, acc_sc, *, tq, tk):
    s_id = pl.program_id(1)
    qi, ki = qi_ref[s_id], ki_ref[s_id]
    rows = MLA_HEADS * tq
    last_k = (qi * tq) // tk

    @pl.when(ki == 0)
    def _():
        m_sc[...] = jnp.full(m_sc.shape, -jnp.inf, _f32)
        l_sc[...] = jnp.zeros(l_sc.shape, _f32)
        acc_sc[...] = jnp.zeros(acc_sc.shape, _f32)

    q = q_ref[...].reshape(rows, MLA_QK)
    k = k_ref[0]
    s = _dot_nt(q, k)

    def update(s):
        m_prev = m_sc[...]
        m_new = jnp.maximum(m_prev, jnp.max(s, axis=-1, keepdims=True))
        alpha = jnp.exp(m_prev - m_new)
        p = jnp.exp(s - m_new)
        l_sc[...] = alpha * l_sc[...] + jnp.sum(p, axis=-1, keepdims=True)
        acc_sc[...] = alpha * acc_sc[...] + _dot(p.astype(_bf16), k[:, :MLA_KV_LORA])
        m_sc[...] = m_new

    @pl.when(ki < last_k)
    def _():
        update(s)

    @pl.when(ki == last_k)
    def _():
        qpos = qi * tq + lax.broadcasted_iota(jnp.int32, (rows, tk), 0) % tq
        kpos = ki * tk + lax.broadcasted_iota(jnp.int32, (rows, tk), 1)
        update(jnp.where(kpos <= qpos, s, -jnp.inf))
        o = acc_sc[...] / l_sc[...]
        for p in range(MLA_HEADS // 2):
            pair = jnp.concatenate([o[(2 * p) * tq:(2 * p + 1) * tq], o[(2 * p + 1) * tq:(2 * p + 2) * tq]], axis=1)
            o_ref[0, :, p * LANES:(p + 1) * LANES] = _dot(pair.astype(_bf16), wuv_ref[p]).astype(o_ref.dtype)


def _mla_prompt(qcat, kcat, w_uvp, bsz, seq):
    tq, tk = min(MLA_TQ, seq), min(MLA_TK, seq)
    assert seq % tq == 0 and seq % tk == 0 and tk % tq == 0
    nq = seq // tq
    pairs = [(q, k) for q in range(nq) for k in range((q * tq) // tk + 1)]
    qi = jnp.asarray([p[0] for p in pairs], jnp.int32)
    ki = jnp.asarray([p[1] for p in pairs], jnp.int32)
    rows = MLA_HEADS * tq
    grid_spec = pltpu.PrefetchScalarGridSpec(
        num_scalar_prefetch=2, grid=(bsz, len(pairs)),
        in_specs=[pl.BlockSpec((MLA_HEADS, tq, MLA_QK), lambda b, s, qi, ki: (0, b * nq + qi[s], 0)),
                  pl.BlockSpec((1, tk, MLA_QK), lambda b, s, qi, ki: (b, ki[s], 0)),
                  pl.BlockSpec(w_uvp.shape, lambda b, s, qi, ki: (0, 0, 0))],
        out_specs=pl.BlockSpec((1, tq, MLA_HEADS * MLA_D_V), lambda b, s, qi, ki: (b, qi[s], 0)),
        scratch_shapes=[pltpu.VMEM((rows, 1), _f32), pltpu.VMEM((rows, 1), _f32),
                        pltpu.VMEM((rows, MLA_KV_LORA), _f32)])
    return pl.pallas_call(
        functools.partial(_mla_prompt_kernel, tq=tq, tk=tk), grid_spec=grid_spec,
        out_shape=jax.ShapeDtypeStruct((bsz, seq, MLA_HEADS * MLA_D_V), _bf16),
        compiler_params=pltpu.CompilerParams(dimension_semantics=("parallel", "arbitrary"),
                                             vmem_limit_bytes=VMEM_LIMIT),
    )(qi, ki, qcat, kcat.reshape(bsz, seq, MLA_QK), w_uvp)


def _top_blocks(gate, lane, n_valid, first_lane):
    g = jnp.where((lane >= first_lane) & (lane < first_lane + n_valid), gate, -jnp.inf)
    chosen = jnp.zeros(gate.shape, jnp.bool_)
    for _ in range(MOBA_TOPK):
        mx = jnp.max(g, axis=-1, keepdims=True)
        idx = jnp.min(jnp.where(g == mx, lane, jnp.int32(2 ** 30)), axis=-1, keepdims=True)
        pick = (lane == idx) & (mx > -jnp.inf)
        chosen = chosen | pick
        g = jnp.where(pick, -jnp.inf, g)
    return chosen


def _moba_prompt_kernel(qi_ref, kt_ref, q_ref, kaug_ref, v_ref, kmean_ref, o_ref, qaug_sc, m_sc, l_sc, acc_sc, *, tk):
    s_id = pl.program_id(1)
    j, kt = qi_ref[s_id], kt_ref[s_id]
    t = MOBA_BLOCK
    nh = MOBA_HEADS

    @pl.when(kt == 0)
    def _():
        lane = lax.broadcasted_iota(jnp.int32, (t, LANES), 1)
        for h in range(nh):
            q = q_ref[h]
            gate = _dot_nt(q, kmean_ref[0, h])
            chosen = _top_blocks(gate, lane, j, MOBA_HEAD_DIM) | (lane == MOBA_HEAD_DIM + j)
            bias = jnp.where(chosen, 0.0, NEG_BIG)
            q_wide = jnp.concatenate([q.astype(_f32), jnp.zeros((t, MOBA_HEAD_DIM), _f32)], axis=1)
            qaug_sc[h] = jnp.where(lane < MOBA_HEAD_DIM, q_wide, bias).astype(_bf16)
        m_sc[...] = jnp.full(m_sc.shape, -jnp.inf, _f32)
        l_sc[...] = jnp.zeros(l_sc.shape, _f32)
        acc_sc[...] = jnp.zeros(acc_sc.shape, _f32)

    qpos = j * t + lax.broadcasted_iota(jnp.int32, (t, tk), 0)
    kpos = kt * tk + lax.broadcasted_iota(jnp.int32, (t, tk), 1)
    causal = kpos <= qpos
    for h in range(nh):
        s = jnp.where(causal, _dot_nt(qaug_sc[h], kaug_ref[h]), -jnp.inf)
        m_prev = m_sc[h]
        m_new = jnp.maximum(m_prev, jnp.max(s, axis=-1, keepdims=True))
        alpha = jnp.exp(m_prev - m_new)
        p = jnp.exp(s - m_new)
        l_sc[h] = alpha * l_sc[h] + jnp.sum(p, axis=-1, keepdims=True)
        acc_sc[h] = alpha * acc_sc[h] + _dot(p.astype(_bf16), v_ref[h])
        m_sc[h] = m_new

    @pl.when(kt == (j * t) // tk)
    def _():
        o_ref[0] = jnp.concatenate([acc_sc[h] / l_sc[h] for h in range(nh)], axis=1).astype(o_ref.dtype)


def _moba_prompt(qbh, kaug, vbh, kmean, bsz, seq):
    t = MOBA_BLOCK
    tk = min(MOBA_TK, seq)
    assert seq % tk == 0 and tk % t == 0
    nb = seq // t
    assert nb <= LANES - MOBA_HEAD_DIM
    nh = MOBA_HEADS
    km = kmean.reshape(bsz, nb, nh, MOBA_HEAD_DIM).transpose(0, 2, 1, 3)
    km = jnp.pad(km, ((0, 0), (0, 0), (MOBA_HEAD_DIM, LANES - MOBA_HEAD_DIM - nb), (0, 0))).astype(_bf16)
    pairs = [(q, k) for q in range(nb) for k in range((q * t) // tk + 1)]
    qi = jnp.asarray([p[0] for p in pairs], jnp.int32)
    kt = jnp.asarray([p[1] for p in pairs], jnp.int32)
    ntk = seq // tk
    qmap = lambda b, s, qi, kt: (0, b * nb + qi[s], 0)
    kmap = lambda b, s, qi, kt: (0, b * ntk + kt[s], 0)
    grid_spec = pltpu.PrefetchScalarGridSpec(
        num_scalar_prefetch=2, grid=(bsz, len(pairs)),
        in_specs=[pl.BlockSpec((nh, t, MOBA_HEAD_DIM), qmap),
                  pl.BlockSpec((nh, tk, LANES), kmap),
                  pl.BlockSpec((nh, tk, MOBA_HEAD_DIM), kmap),
                  pl.BlockSpec((1, nh, LANES, MOBA_HEAD_DIM), lambda b, s, qi, kt: (b, 0, 0, 0))],
        out_specs=pl.BlockSpec((1, t, MOBA_WIDTH), lambda b, s, qi, kt: (b, qi[s], 0)),
        scratch_shapes=[pltpu.VMEM((nh, t, LANES), _bf16), pltpu.VMEM((nh, t, 1), _f32),
                        pltpu.VMEM((nh, t, 1), _f32), pltpu.VMEM((nh, t, MOBA_HEAD_DIM), _f32)])
    return pl.pallas_call(
        functools.partial(_moba_prompt_kernel, tk=tk), grid_spec=grid_spec,
        out_shape=jax.ShapeDtypeStruct((bsz, seq, MOBA_WIDTH), _bf16),
        compiler_params=pltpu.CompilerParams(dimension_semantics=("parallel", "arbitrary"),
                                             vmem_limit_bytes=VMEM_LIMIT),
    )(qi, kt, qbh, kaug, vbh, km)


def _prep_attn_weights(g_attn, w_in, g_q, w_uq, g_kv, w_uk, w_uv):
    c0, c1, c2 = MLA_Q_LORA, MLA_Q_LORA + MLA_KV_LORA, MLA_Q_LORA + MLA_KV_LORA + MLA_D_ROPE
    w_in_r = jnp.concatenate([w_in[:, :c1], w_in[:, c2:], w_in[:, c1:c2],
                              jnp.zeros((D_MODEL, LANES - MLA_D_ROPE), w_in.dtype)], axis=1)
    w_uqn = w_uq[:, :, :MLA_D_NOPE].reshape(MLA_Q_LORA, MLA_HEADS * MLA_D_NOPE)
    w_uqr = w_uq[:, :, MLA_D_NOPE:].reshape(MLA_Q_LORA, MLA_HEADS * MLA_D_ROPE)
    ukt = jnp.transpose(w_uk, (1, 2, 0))
    zk = jnp.zeros_like(ukt[0])
    w_ukp = jnp.stack([jnp.concatenate([jnp.concatenate([ukt[2 * p], zk], axis=1),
                                        jnp.concatenate([zk, ukt[2 * p + 1]], axis=1)], axis=0)
                       for p in range(MLA_HEADS // 2)])
    uv = jnp.transpose(w_uv, (1, 0, 2))
    zv = jnp.zeros_like(uv[0])
    w_uvp = jnp.stack([jnp.concatenate([jnp.concatenate([uv[2 * p], zv], axis=1),
                                        jnp.concatenate([zv, uv[2 * p + 1]], axis=1)], axis=0)
                       for p in range(MLA_HEADS // 2)])
    return {'g_attn': g_attn.reshape(1, -1), 'w_in': w_in_r.astype(_bf16), 'g_q': g_q.reshape(1, -1),
            'w_uqn': w_uqn.astype(_bf16), 'w_uqr': w_uqr.astype(_bf16), 'g_kv': g_kv.reshape(1, -1),
            'w_ukp': w_ukp.astype(_bf16), 'w_uvp': w_uvp.astype(_bf16)}


def _prep_post_weights(w_out, g_ffn, w_gr, b_gr, w_er, b_er, g_ple, w_ple_gate, w_ple_proj, g_final):
    pad = LANES - MOE_EXPERTS - MOE_GROUPS
    w_r = jnp.concatenate([w_er, w_gr, jnp.zeros((D_MODEL, pad), w_er.dtype)], axis=1)
    b_r = jnp.concatenate([b_er, b_gr, jnp.zeros((pad,), b_er.dtype)]).reshape(1, LANES)
    w_r_hi = w_r.astype(_bf16)
    w_r_lo = (w_r - w_r_hi.astype(_f32)).astype(_bf16)
    return {'w_out': w_out.astype(_bf16), 'g_ffn': g_ffn.reshape(1, -1), 'w_r_hi': w_r_hi, 'w_r_lo': w_r_lo,
            'b_r': b_r, 'g_ple': g_ple.reshape(1, -1), 'w_ple_gate': w_ple_gate.astype(_bf16),
            'w_ple_proj': w_ple_proj.astype(_bf16), 'g_final': g_final.reshape(1, -1)}


def _route_kernel(x_ref, omla_ref, omoba_ref, w_out_ref, g_ffn_ref, wr_hi_ref, wr_lo_ref, br_ref,
                  x1_ref, h2_ref, ri_ref, rw_ref, cnt_ref, cnt_sc):
    tm = x_ref.shape[0]

    @pl.when(pl.program_id(0) == 0)
    def _():
        cnt_sc[...] = jnp.zeros(cnt_sc.shape, _f32)

    om = jnp.concatenate([omla_ref[...], omoba_ref[...]], axis=1)
    x1 = x_ref[...] + _dot(om, w_out_ref[...])
    x1_ref[...] = x1
    h2 = _rms(x1, g_ffn_ref[...])
    h2_ref[...] = h2
    hi = h2.astype(_bf16)
    lo = (h2 - hi.astype(_f32)).astype(_bf16)
    logits = _dot(hi, wr_hi_ref[...]) + _dot(lo, wr_hi_ref[...]) + _dot(hi, wr_lo_ref[...]) + br_ref[...]

    lane = lax.broadcasted_iota(jnp.int32, (tm, LANES), 1)
    big = jnp.int32(2 ** 30)
    first = lambda mask: jnp.min(jnp.where(mask, lane, big), axis=-1, keepdims=True)
    gl = jnp.where((lane >= MOE_EXPERTS) & (lane < MOE_EXPERTS + MOE_GROUPS), logits, -jnp.inf)
    gmax = jnp.max(gl, axis=-1, keepdims=True)
    g_w = 1.0 / jnp.sum(jnp.exp(gl - gmax), axis=-1, keepdims=True)
    g_idx = first(gl == gmax) - MOE_EXPERTS
    e0 = g_idx * MOE_EXPERTS_PER_GROUP
    el = jnp.where((lane >= e0) & (lane < e0 + MOE_EXPERTS_PER_GROUP), logits, -jnp.inf)
    emax = jnp.max(el, axis=-1, keepdims=True)
    esum = jnp.sum(jnp.exp(el - emax), axis=-1, keepdims=True)
    i1 = first(el == emax)
    el2 = jnp.where(lane == i1, -jnp.inf, el)
    e2max = jnp.max(el2, axis=-1, keepdims=True)
    i2 = first(el2 == e2max)
    p1 = 1.0 / esum
    p2 = jnp.exp(e2max - emax) / esum
    w1 = g_w * p1 / (p1 + p2)
    w2 = g_w * p2 / (p1 + p2)
    oh1, oh2 = lane == i1, lane == i2
    both = jnp.where(oh1 | oh2, 1.0, 0.0)
    rowi = lax.broadcasted_iota(jnp.int32, (tm, tm), 0)
    coli = lax.broadcasted_iota(jnp.int32, (tm, tm), 1)
    tri = jnp.where(coli < rowi, 1.0, 0.0).astype(_bf16)
    before = cnt_sc[...] + _dot(tri, both.astype(_bf16))
    r1 = jnp.sum(jnp.where(oh1, before, 0.0), axis=-1, keepdims=True).astype(jnp.int32)
    r2 = jnp.sum(jnp.where(oh2, before, 0.0), axis=-1, keepdims=True).astype(jnp.int32)
    cnt = cnt_sc[...] + jnp.sum(both, axis=0, keepdims=True)
    cnt_sc[...] = cnt
    cnt_ref[...] = cnt
    l8 = lax.broadcasted_iota(jnp.int32, (tm, 8), 1)
    ri_ref[...] = jnp.where(l8 == 0, i1, jnp.where(l8 == 1, i2, jnp.where(l8 == 2, r1, jnp.where(l8 == 3, r2, 0))))
    rw_ref[...] = jnp.where(l8 == 0, w1, jnp.where(l8 == 1, w2, 0.0))


def _route(x, o_mla, o_moba, w):
    n = x.shape[0]
    tm = ROW_TILE
    row = lambda wd: pl.BlockSpec((tm, wd), lambda i: (i, 0))
    full = lambda a: pl.BlockSpec(a.shape, lambda i: (0,) * a.ndim)
    ws = [w['w_out'], w['g_ffn'], w['w_r_hi'], w['w_r_lo'], w['b_r']]
    return pl.pallas_call(
        _route_kernel, grid=(n // tm,),
        in_specs=[row(D_MODEL), row(MLA_HEADS * MLA_D_V), row(MOBA_WIDTH)] + [full(a) for a in ws],
        out_specs=[row(D_MODEL), row(D_MODEL), row(8), row(8), pl.BlockSpec((1, LANES), lambda i: (0, 0))],
        out_shape=[jax.ShapeDtypeStruct((n, D_MODEL), _f32), jax.ShapeDtypeStruct((n, D_MODEL), _f32),
                   jax.ShapeDtypeStruct((n, 8), jnp.int32), jax.ShapeDtypeStruct((n, 8), _f32),
                   jax.ShapeDtypeStruct((1, LANES), _f32)],
        scratch_shapes=[pltpu.VMEM((1, LANES), _f32)],
        compiler_params=pltpu.CompilerParams(dimension_semantics=("arbitrary",), vmem_limit_bytes=VMEM_LIMIT),
    )(x, o_mla, o_moba, *ws)


def _scatter_kernel(dest_ref, h_ref, xs_in_ref, xs_ref, sem):
    del xs_in_ref
    tm = h_ref.shape[0]

    def copy(r, k):
        return pltpu.make_async_copy(h_ref.at[pl.ds(r, 1)], xs_ref.at[pl.ds(dest_ref[2 * r + k], 1)], sem)

    def start(r, c):
        copy(r, 0).start()
        copy(r, 1).start()
        return c

    def wait(r, c):
        copy(r, 0).wait()
        copy(r, 1).wait()
        return c

    lax.fori_loop(0, tm, start, 0, unroll=8)
    lax.fori_loop(0, tm, wait, 0, unroll=8)


def _scatter_rows(h2, dest, rows):
    n = h2.shape[0]
    tm = ROW_TILE
    return pl.pallas_call(
        _scatter_kernel, grid=(n // tm,),
        in_specs=[pl.BlockSpec((2 * tm,), lambda i: (i,), memory_space=pltpu.SMEM),
                  pl.BlockSpec((tm, D_MODEL), lambda i: (i, 0)),
                  pl.BlockSpec(memory_space=pl.ANY)],
        out_specs=pl.BlockSpec(memory_space=pl.ANY),
        out_shape=jax.ShapeDtypeStruct((rows, D_MODEL), h2.dtype),
        scratch_shapes=[pltpu.SemaphoreType.DMA(())],
        input_output_aliases={2: 0},
        compiler_params=pltpu.CompilerParams(dimension_semantics=("arbitrary",), has_side_effects=True),
    )(dest.reshape(-1), h2, jnp.zeros((rows, D_MODEL), h2.dtype))


def _expert_kernel(be_ref, na_ref, xs_ref, wg_ref, wu_ref, wd_ref, yb_ref, wg_sc, wu_sc, wd_sc):
    b = pl.program_id(0)

    @pl.when(b < na_ref[0])
    def _():
        prev = be_ref[jnp.maximum(b - 1, 0)]

        @pl.when((b == 0) | (be_ref[b] != prev))
        def _():
            wg_sc[...] = wg_ref[0].astype(_bf16)
            wu_sc[...] = wu_ref[0].astype(_bf16)
            wd_sc[...] = wd_ref[0].astype(_bf16)

        xi = xs_ref[...].astype(_bf16)
        g = _dot(xi, wg_sc[...])
        u = _dot(xi, wu_sc[...])
        a = g * (1.0 / (1.0 + jnp.exp(-g))) * u
        yb_ref[...] = _dot(a.astype(_bf16), wd_sc[...])

    @pl.when(b >= na_ref[0])
    def _():
        yb_ref[...] = jnp.zeros(yb_ref.shape, yb_ref.dtype)


def _experts(xs, blk_exp, n_active, w_eg, w_eu, w_ed):
    rows = xs.shape[0]
    r = MOE_ROWS
    nblk = rows // r
    blk = lambda b, be, na: (jnp.minimum(b, na[0] - 1), 0)
    wmap = lambda b, be, na: (be[jnp.minimum(b, na[0] - 1)], 0, 0)
    grid_spec = pltpu.PrefetchScalarGridSpec(
        num_scalar_prefetch=2, grid=(nblk,),
        in_specs=[pl.BlockSpec((r, D_MODEL), blk),
                  pl.BlockSpec((1, D_MODEL, D_EXPERT), wmap), pl.BlockSpec((1, D_MODEL, D_EXPERT), wmap),
                  pl.BlockSpec((1, D_EXPERT, D_MODEL), wmap)],
        out_specs=pl.BlockSpec((r, D_MODEL), lambda b, be, na: (b, 0)),
        scratch_shapes=[pltpu.VMEM((D_MODEL, D_EXPERT), _bf16), pltpu.VMEM((D_MODEL, D_EXPERT), _bf16),
                        pltpu.VMEM((D_EXPERT, D_MODEL), _bf16)])
    return pl.pallas_call(
        _expert_kernel, grid_spec=grid_spec, out_shape=jax.ShapeDtypeStruct((rows, D_MODEL), _f32),
        compiler_params=pltpu.CompilerParams(dimension_semantics=("arbitrary",), vmem_limit_bytes=VMEM_LIMIT),
    )(blk_exp, n_active, xs, w_eg, w_eu, w_ed)


def _combine_kernel(dest_ref, yb_ref, x1_ref, rw_ref, p_ref, g_ple_ref, w_pg_ref, w_pp_ref, g_fin_ref,
                    y_ref, ybuf, sem, *, final_norm):
    tm = x1_ref.shape[0]

    def copy(r, k):
        return pltpu.make_async_copy(yb_ref.at[pl.ds(dest_ref[2 * r + k], 1)], ybuf.at[k, pl.ds(r, 1)], sem)

    def start(r, c):
        copy(r, 0).start()
        copy(r, 1).start()
        return c

    def wait(r, c):
        copy(r, 0).wait()
        copy(r, 1).wait()
        return c

    lax.fori_loop(0, tm, start, 0, unroll=8)
    ple = _dot(p_ref[...].astype(_bf16), w_pp_ref[...])
    lax.fori_loop(0, tm, wait, 0, unroll=8)
    rw = rw_ref[...]
    x2 = x1_ref[...] + (ybuf[0] * rw[:, 0:1] + ybuf[1] * rw[:, 1:2])
    gate = _dot(_rms(x2, g_ple_ref[...]).astype(_bf16), w_pg_ref[...])
    x3 = x2 + ple * (1.0 / (1.0 + jnp.exp(-gate)))
    y_ref[...] = _rms(x3, g_fin_ref[...]) if final_norm else x3


def _combine(yb, dest, x1, rw, p, w, final_norm):
    n = x1.shape[0]
    tm = ROW_TILE
    row = lambda wd: pl.BlockSpec((tm, wd), lambda i: (i, 0))
    full = lambda a: pl.BlockSpec(a.shape, lambda i: (0,) * a.ndim)
    ws = [w['g_ple'], w['w_ple_gate'], w['w_ple_proj'], w['g_final']]
    return pl.pallas_call(
        functools.partial(_combine_kernel, final_norm=final_norm), grid=(n // tm,),
        in_specs=[pl.BlockSpec((2 * tm,), lambda i: (i,), memory_space=pltpu.SMEM),
                  pl.BlockSpec(memory_space=pl.ANY), row(D_MODEL), row(8), row(PLE_DIM)] + [full(a) for a in ws],
        out_specs=row(D_MODEL),
        out_shape=jax.ShapeDtypeStruct((n, D_MODEL), _f32),
        scratch_shapes=[pltpu.VMEM((2, tm, D_MODEL), _f32), pltpu.SemaphoreType.DMA(())],
        compiler_params=pltpu.CompilerParams(dimension_semantics=("arbitrary",), vmem_limit_bytes=VMEM_LIMIT),
    )(dest.reshape(-1), yb, x1, rw, p, *ws)


def _post(x, o_mla, o_moba, p, w, w_eg, w_eu, w_ed, final_norm=True):
    n = x.shape[0]
    x1, h2, ri, rw, cnt = _route(x, o_mla, o_moba, w)
    counts = cnt[0, :MOE_EXPERTS].astype(jnp.int32)
    r = MOE_ROWS
    padded = (counts + r - 1) // r * r
    pad_end = jnp.cumsum(padded)
    pad_start = pad_end - padded
    dest = pad_start[ri[:, 0:2]] + ri[:, 2:4]
    nblk = -(-(2 * n) // r) + MOE_EXPERTS
    blk_exp = jnp.minimum(jnp.searchsorted(pad_end, jnp.arange(nblk, dtype=jnp.int32) * r, side='right'),
                          MOE_EXPERTS - 1).astype(jnp.int32)
    n_active = (pad_end[-1:] // r).astype(jnp.int32)
    xs = _scatter_rows(h2, dest, nblk * r)
    yb = _experts(xs, blk_exp, n_active, w_eg, w_eu, w_ed)
    return _combine(yb, dest, x1, rw, p, w, final_norm)


def _paged_step(n_steps_inner):
    step = pl.program_id(0) * n_steps_inner + pl.program_id(1)
    return step, step % 2


def _run_paged(copies, step, slot, n_inner):
    total = pl.num_programs(0) * n_inner

    @pl.when(step == 0)
    def _():
        for cp in copies(pl.program_id(0), pl.program_id(1), slot):
            cp.start()

    @pl.when(step + 1 < total)
    def _():
        nxt = step + 1
        for cp in copies(nxt // n_inner, nxt % n_inner, 1 - slot):
            cp.start()

    for cp in copies(pl.program_id(0), pl.program_id(1), slot):
        cp.wait()


def _mla_sample_kernel(pt_ref, q_ref, knew_ref, ckv_hbm, kpet_hbm, o_ref, cbuf, pbuf, m_sc, l_sc, acc_sc, sem,
                       *, cp, nch, dt):
    c = pl.program_id(1)
    step, slot = _paged_step(nch)

    def copies(bb, cc, sl):
        out = []
        for j in range(cp):
            page = pt_ref[bb, cc * cp + j]
            out.append(pltpu.make_async_copy(ckv_hbm.at[page], cbuf.at[sl, j], sem.at[0, sl]))
            out.append(pltpu.make_async_copy(kpet_hbm.at[page], pbuf.at[sl, j], sem.at[1, sl]))
        return out

    _run_paged(copies, step, slot, nch)

    @pl.when(c == 0)
    def _():
        m_sc[...] = jnp.full(m_sc.shape, -jnp.inf, _f32)
        l_sc[...] = jnp.zeros(l_sc.shape, _f32)
        acc_sc[...] = jnp.zeros(acc_sc.shape, _f32)

    def update(s, v):
        m_prev = m_sc[...]
        m_new = jnp.maximum(m_prev, jnp.max(s, axis=-1, keepdims=True))
        alpha = jnp.exp(m_prev - m_new)
        p = jnp.exp(s - m_new)
        l_sc[...] = alpha * l_sc[...] + jnp.sum(p, axis=-1, keepdims=True)
        acc_sc[...] = alpha * acc_sc[...] + _dot(p.astype(_bf16), v)
        m_sc[...] = m_new

    q = q_ref[0]
    q_lat, q_pe = q[:, :MLA_KV_LORA], q[:, MLA_KV_LORA:]
    kc = cbuf[slot].reshape(cp * PAGE_SIZE, MLA_KV_LORA).astype(_bf16)
    s_pe = jnp.concatenate([_dot(q_pe, pbuf[slot, j].astype(_bf16)) for j in range(cp)], axis=1)
    update(_dot_nt(q_lat, kc) + s_pe, kc)

    @pl.when(c == nch - 1)
    def _():
        knew = knew_ref[0]
        s_new = _dot_nt(q, knew)
        t_row = lax.broadcasted_iota(jnp.int32, s_new.shape, 0) % dt
        col = lax.broadcasted_iota(jnp.int32, s_new.shape, 1)
        update(jnp.where(col <= t_row, s_new, -jnp.inf), knew[:, :MLA_KV_LORA])
        o_ref[0] = acc_sc[...] / l_sc[...]


def _mla_sample(q_rows, k_new, ckv_pool, kpet_pool, page_table, dt):
    db, npg = page_table.shape
    cp = min(MLA_PAGES, npg)
    assert npg % cp == 0
    nch = npg // cp
    rows = MLA_HEADS * dt
    grid_spec = pltpu.PrefetchScalarGridSpec(
        num_scalar_prefetch=1, grid=(db, nch),
        in_specs=[pl.BlockSpec((1, rows, MLA_QK), lambda b, c, pt: (b, 0, 0)),
                  pl.BlockSpec((1,) + k_new.shape[1:], lambda b, c, pt: (b, 0, 0)),
                  pl.BlockSpec(memory_space=pl.ANY), pl.BlockSpec(memory_space=pl.ANY)],
        out_specs=pl.BlockSpec((1, rows, MLA_KV_LORA), lambda b, c, pt: (b, 0, 0)),
        scratch_shapes=[pltpu.VMEM((2, cp, PAGE_SIZE, MLA_KV_LORA), _f32),
                        pltpu.VMEM((2, cp, MLA_D_ROPE, PAGE_SIZE), _f32),
                        pltpu.VMEM((rows, 1), _f32), pltpu.VMEM((rows, 1), _f32),
                        pltpu.VMEM((rows, MLA_KV_LORA), _f32), pltpu.SemaphoreType.DMA((2, 2))])
    return pl.pallas_call(
        functools.partial(_mla_sample_kernel, cp=cp, nch=nch, dt=dt), grid_spec=grid_spec,
        out_shape=jax.ShapeDtypeStruct((db, rows, MLA_KV_LORA), _f32),
        compiler_params=pltpu.CompilerParams(dimension_semantics=("arbitrary", "arbitrary"),
                                             vmem_limit_bytes=VMEM_LIMIT),
    )(page_table, q_rows, k_new, ckv_pool, kpet_pool)


def _uv_kernel(o_ref, w_ref, out_ref):
    for p in range(MLA_HEADS // 2):
        pair = o_ref[:, 2 * p * MLA_KV_LORA:(2 * p + 2) * MLA_KV_LORA].astype(_bf16)
        out_ref[:, p * LANES:(p + 1) * LANES] = _dot(pair, w_ref[p]).astype(out_ref.dtype)


def _uv_project(o_tok, w_uvp):
    n = o_tok.shape[0]
    tm = ROW_TILE
    return pl.pallas_call(
        _uv_kernel, grid=(n // tm,),
        in_specs=[pl.BlockSpec((tm, MLA_HEADS * MLA_KV_LORA), lambda i: (i, 0)),
                  pl.BlockSpec(w_uvp.shape, lambda i: (0, 0, 0))],
        out_specs=pl.BlockSpec((tm, MLA_HEADS * MLA_D_V), lambda i: (i, 0)),
        out_shape=jax.ShapeDtypeStruct((n, MLA_HEADS * MLA_D_V), _bf16),
    )(o_tok, w_uvp)


def _top_idx(gate, lane, n_valid):
    g = jnp.where(lane < n_valid, gate, -jnp.inf)
    out = []
    for _ in range(MOBA_TOPK):
        mx = jnp.max(g, axis=-1, keepdims=True)
        idx = jnp.min(jnp.where(g == mx, lane, jnp.int32(2 ** 30)), axis=-1, keepdims=True)
        out.append(idx)
        g = jnp.where(lane == idx, -jnp.inf, g)
    return out


def _moba_select_kernel(pt_ref, qt_ref, kt_hbm, idx_ref, kbuf, ksum_sc, sem, *, cp, nch, dt, nb):
    c = pl.program_id(1)
    step, slot = _paged_step(nch)
    ppb = MOBA_BLOCK // PAGE_SIZE

    def copies(bb, cc, sl):
        return [pltpu.make_async_copy(kt_hbm.at[pt_ref[bb, cc * cp + j]], kbuf.at[sl, j], sem.at[sl])
                for j in range(cp)]

    _run_paged(copies, step, slot, nch)

    @pl.when(c == 0)
    def _():
        ksum_sc[...] = jnp.zeros(ksum_sc.shape, _f32)

    lane = lax.broadcasted_iota(jnp.int32, (MOBA_HEAD_DIM, LANES), 1)
    for h in range(MOBA_HEADS):
        acc = ksum_sc[h]
        for jb in range(cp // ppb):
            blk = kbuf[slot, ppb * jb, h]
            for pg in range(1, ppb):
                blk = blk + kbuf[slot, ppb * jb + pg, h]
            red = jnp.sum(blk, axis=-1, keepdims=True)
            acc = jnp.where(lane == c * (cp // ppb) + jb, red, acc)
        ksum_sc[h] = acc

    @pl.when(c == nch - 1)
    def _():
        lane_t = lax.broadcasted_iota(jnp.int32, (dt, LANES), 1)
        for h in range(MOBA_HEADS):
            km = ksum_sc[h] * (1.0 / MOBA_BLOCK)
            qt = qt_ref[0, h]
            gate = jnp.concatenate([jnp.sum(km * qt[:, t:t + 1], axis=0, keepdims=True) for t in range(dt)], axis=0)
            i0, i1, i2 = _top_idx(gate, lane_t, nb)
            idx_ref[0, h] = jnp.where(lane_t == 0, i0, jnp.where(lane_t == 1, i1, i2))


def _moba_select(qt, kt_pool, page_table, dt):
    db, npg = page_table.shape
    ppb = MOBA_BLOCK // PAGE_SIZE
    cp = min(MOBA_PAGES, npg)
    assert npg % cp == 0 and cp % ppb == 0
    nch = npg // cp
    nb = npg // ppb
    assert MOBA_TOPK <= nb <= LANES
    grid_spec = pltpu.PrefetchScalarGridSpec(
        num_scalar_prefetch=1, grid=(db, nch),
        in_specs=[pl.BlockSpec((1, MOBA_HEADS, MOBA_HEAD_DIM, dt), lambda b, c, pt: (b, 0, 0, 0)),
                  pl.BlockSpec(memory_space=pl.ANY)],
        out_specs=pl.BlockSpec((1, MOBA_HEADS, dt, LANES), lambda b, c, pt: (b, 0, 0, 0)),
        scratch_shapes=[pltpu.VMEM((2, cp, MOBA_HEADS, MOBA_HEAD_DIM, PAGE_SIZE), _f32),
                        pltpu.VMEM((MOBA_HEADS, MOBA_HEAD_DIM, LANES), _f32), pltpu.SemaphoreType.DMA((2,))])
    return pl.pallas_call(
        functools.partial(_moba_select_kernel, cp=cp, nch=nch, dt=dt, nb=nb), grid_spec=grid_spec,
        out_shape=jax.ShapeDtypeStruct((db, MOBA_HEADS, dt, LANES), jnp.int32),
        compiler_params=pltpu.CompilerParams(dimension_semantics=("arbitrary", "arbitrary"),
                                             vmem_limit_bytes=VMEM_LIMIT),
    )(page_table, qt, kt_pool)


def _moba_sample_kernel(pt_ref, sel_ref, qt_ref, knt_ref, vnt_ref, kt_hbm, vt_hbm, o_ref, kbuf, vbuf, sem,
                        *, hg, nhg, dt):
    g = pl.program_id(1)
    step, slot = _paged_step(nhg)
    ppb = MOBA_BLOCK // PAGE_SIZE
    per_q = MOBA_TOPK * ppb

    def copies(bb, gg, sl):
        out = []
        for hl in range(hg):
            h = gg * hg + hl
            for t in range(dt):
                for k in range(MOBA_TOPK):
                    blk = sel_ref[((bb * MOBA_HEADS + h) * dt + t) * MOBA_TOPK + k]
                    for pg in range(ppb):
                        page = pt_ref[bb, blk * ppb + pg]
                        i = (hl * dt + t) * per_q + k * ppb + pg
                        out.append(pltpu.make_async_copy(kt_hbm.at[page, h], kbuf.at[sl, i], sem.at[0, sl]))
                        out.append(pltpu.make_async_copy(vt_hbm.at[page, h], vbuf.at[sl, i], sem.at[1, sl]))
        return out

    _run_paged(copies, step, slot, nhg)

    lane_t = lax.broadcasted_iota(jnp.int32, (1, dt), 1)
    lane_o = lax.broadcasted_iota(jnp.int32, (MOBA_HEAD_DIM, dt), 1)
    for hl in range(hg):
        qt, knt, vnt = qt_ref[0, hl], knt_ref[0, hl], vnt_ref[0, hl]
        o_cols = jnp.zeros((MOBA_HEAD_DIM, dt), _f32)
        for t in range(dt):
            qcol = qt[:, t:t + 1]
            base = (hl * dt + t) * per_q
            s_sel = [jnp.sum(kbuf[slot, base + r] * qcol, axis=0, keepdims=True) for r in range(per_q)]
            s_own = jnp.where(lane_t <= t, jnp.sum(knt * qcol, axis=0, keepdims=True), -jnp.inf)
            m_row = s_sel[0]
            for r in range(1, per_q):
                m_row = jnp.maximum(m_row, s_sel[r])
            m = jnp.maximum(jnp.max(m_row, axis=-1, keepdims=True), jnp.max(s_own, axis=-1, keepdims=True))
            e_sel = [jnp.exp(s - m) for s in s_sel]
            e_own = jnp.exp(s_own - m)
            e_row = e_sel[0]
            acc = vbuf[slot, base] * e_sel[0]
            for r in range(1, per_q):
                e_row = e_row + e_sel[r]
                acc = acc + vbuf[slot, base + r] * e_sel[r]
            l = jnp.sum(e_row, axis=-1, keepdims=True) + jnp.sum(e_own, axis=-1, keepdims=True)
            o_col = (jnp.sum(acc, axis=-1, keepdims=True) + jnp.sum(vnt * e_own, axis=-1, keepdims=True)) / l
            o_cols = jnp.where(lane_o == t, o_col, o_cols)
        o_ref[0, hl] = o_cols


def _moba_sample(qt, knt, vnt, sel, kt_pool, vt_pool, page_table, dt):
    db, npg = page_table.shape
    hg = MOBA_HEAD_GROUP
    nhg = MOBA_HEADS // hg
    nslab = hg * dt * MOBA_TOPK * (MOBA_BLOCK // PAGE_SIZE)
    col = pl.BlockSpec((1, hg, MOBA_HEAD_DIM, dt), lambda b, g, pt, sel: (b, g, 0, 0))
    grid_spec = pltpu.PrefetchScalarGridSpec(
        num_scalar_prefetch=2, grid=(db, nhg),
        in_specs=[col, col, col, pl.BlockSpec(memory_space=pl.ANY), pl.BlockSpec(memory_space=pl.ANY)],
        out_specs=col,
        scratch_shapes=[pltpu.VMEM((2, nslab, MOBA_HEAD_DIM, PAGE_SIZE), _f32),
                        pltpu.VMEM((2, nslab, MOBA_HEAD_DIM, PAGE_SIZE), _f32), pltpu.SemaphoreType.DMA((2, 2))])
    return pl.pallas_call(
        functools.partial(_moba_sample_kernel, hg=hg, nhg=nhg, dt=dt), grid_spec=grid_spec,
        out_shape=jax.ShapeDtypeStruct((db, MOBA_HEADS, MOBA_HEAD_DIM, dt), _f32),
        compiler_params=pltpu.CompilerParams(dimension_semantics=("arbitrary", "arbitrary"),
                                             vmem_limit_bytes=VMEM_LIMIT),
    )(page_table, sel, qt, knt, vnt, kt_pool, vt_pool)


def _sample_attention(qcat_s, ckv_s, kpe_s, qb_s, kb_s, vb_s, cache_ckv, cache_kpe, cache_k, cache_v, page_table,
                      w_uvp, db, dt):
    n_s = db * dt
    heads = (db, dt, MOBA_HEADS, MOBA_HEAD_DIM)
    kpet_pool = jnp.transpose(cache_kpe, (0, 2, 1))
    kt_pool = jnp.transpose(cache_k, (0, 2, 3, 1))
    vt_pool = jnp.transpose(cache_v, (0, 2, 3, 1))
    q_rows = qcat_s.reshape(MLA_HEADS, db, dt, MLA_QK).transpose(1, 0, 2, 3).reshape(db, MLA_HEADS * dt, MLA_QK)
    k_new = jnp.concatenate([ckv_s, kpe_s], axis=1).astype(_bf16).reshape(db, dt, MLA_QK)
    assert dt <= 16
    k_new = jnp.pad(k_new, ((0, 0), (0, 16 - dt), (0, 0)))
    o_lat = _mla_sample(q_rows, k_new, cache_ckv, kpet_pool, page_table, dt)
    o_tok = o_lat.reshape(db, MLA_HEADS, dt, MLA_KV_LORA).transpose(0, 2, 1, 3).reshape(n_s, -1)
    o_mla = _uv_project(o_tok, w_uvp)
    cols = lambda a: a.reshape(heads).transpose(0, 2, 3, 1)
    qt, knt, vnt = cols(qb_s), cols(kb_s), cols(vb_s)
    sel = _moba_select(qt, kt_pool, page_table, dt)[..., :MOBA_TOPK].reshape(-1)
    ot = _moba_sample(qt, knt, vnt, sel, kt_pool, vt_pool, page_table, dt)
    o_moba = ot.transpose(0, 3, 1, 2).reshape(n_s, MOBA_WIDTH).astype(_bf16)
    return o_mla, o_moba


def kernel(x_prompt, x_sample, cache_ckv, cache_kpe, cache_k, cache_v, page_table, p_prompt, p_sample, g_attn, w_in,
           g_q, w_uq, g_kv, w_uk, w_uv, w_out, g_ffn, w_group_router, b_group_router, w_expert_router,
           b_expert_router, w_exp_gate, w_exp_up, w_exp_down, g_ple, w_ple_gate, w_ple_proj, g_final):
    bsz, seq, _ = x_prompt.shape
    db, dt, _ = x_sample.shape
    depth = g_attn.shape[0]
    n_p, n_s = bsz * seq, db * dt
    past = page_table.shape[1] * PAGE_SIZE
    assert ROW_TILE % dt == 0 and n_s % ROW_TILE == 0
    pos_p = jnp.arange(seq, dtype=jnp.int32)
    pos_s = past + jnp.arange(ROW_TILE, dtype=jnp.int32) % dt
    xp = x_prompt.reshape(n_p, D_MODEL)
    xs = x_sample.reshape(n_s, D_MODEL)
    outs_p, outs_s = [], []
    for i in range(depth):
        wa = _prep_attn_weights(g_attn[i], w_in[i], g_q[i], w_uq[i], g_kv[i], w_uk[i], w_uv[i])
        wp = _prep_post_weights(w_out[i], g_ffn[i], w_group_router[i], b_group_router[i], w_expert_router[i],
                                b_expert_router[i], g_ple[i], w_ple_gate[i], w_ple_proj[i], g_final)
        ckv, kpe, kb, vb, qcat, kcat, qbh, kaug, vbh, kmean = _project(xp, pos_p, wa, sample=False)
        o_mla = _mla_prompt(qcat, kcat, wa['w_uvp'], bsz, seq).reshape(n_p, MLA_HEADS * MLA_D_V)
        o_moba = _moba_prompt(qbh, kaug, vbh, kmean, bsz, seq).reshape(n_p, MOBA_WIDTH)
        outs_p.append((ckv, kpe, kb, vb))
        ckv_s, kpe_s, kb_s, vb_s, qcat_s, qb_s = _project(xs, pos_s, wa, sample=True)
        o_mla_s, o_moba_s = _sample_attention(qcat_s, ckv_s, kpe_s, qb_s, kb_s, vb_s, cache_ckv[i], cache_kpe[i],
                                              cache_k[i], cache_v[i], page_table, wa['w_uvp'], db, dt)
        outs_s.append((ckv_s, kpe_s, kb_s, vb_s))
        x_all = jnp.concatenate([xp, xs], axis=0)
        y_all = _post(x_all, jnp.concatenate([o_mla, o_mla_s], axis=0), jnp.concatenate([o_moba, o_moba_s], axis=0),
                      jnp.concatenate([p_prompt[i].reshape(n_p, PLE_DIM), p_sample[i].reshape(n_s, PLE_DIM)], axis=0),
                      wp, w_exp_gate[i], w_exp_up[i], w_exp_down[i], final_norm=(i == depth - 1))
        xp, xs = y_all[:n_p], y_all[n_p:]
    stack = lambda outs, j, shp: jnp.stack([o[j].reshape(shp) for o in outs])
    hs = (MOBA_HEADS, MOBA_HEAD_DIM)
    return (xp.reshape(bsz, seq, D_MODEL), xs.reshape(db, dt, D_MODEL),
            stack(outs_p, 0, (bsz, seq, MLA_KV_LORA)), stack(outs_p, 1, (bsz, seq, MLA_D_ROPE)),
            stack(outs_p, 2, (bsz, seq) + hs), stack(outs_p, 3, (bsz, seq) + hs),
            stack(outs_s, 0, (db, dt, MLA_KV_LORA)), stack(outs_s, 1, (db, dt, MLA_D_ROPE)),
            stack(outs_s, 2, (db, dt) + hs), stack(outs_s, 3, (db, dt) + hs))
```

```python
import functools

import jax
import jax.numpy as jnp
import numpy as np
from jax import lax
from jax.experimental import pallas as pl
from jax.experimental.pallas import tpu as pltpu

D_MODEL = 1024
PAGE_SIZE = 128
MLA_HEADS = 8
MLA_Q_LORA = 256
MLA_KV_LORA = 128
MLA_D_NOPE = 64
MLA_D_ROPE = 32
MLA_D_V = 64
MLA_SCALE = (MLA_D_NOPE + MLA_D_ROPE) ** -0.5
MLA_QK = MLA_KV_LORA + MLA_D_ROPE
MOBA_HEADS = 8
MOBA_HEAD_DIM = 64
MOBA_WIDTH = MOBA_HEADS * MOBA_HEAD_DIM
MOBA_BLOCK = 256
MOBA_TOPK = 3
MOBA_SCALE = MOBA_HEAD_DIM ** -0.5
MOE_GROUPS = 4
MOE_EXPERTS_PER_GROUP = 8
MOE_EXPERTS = MOE_GROUPS * MOE_EXPERTS_PER_GROUP
D_EXPERT = 512
PLE_DIM = 256
ROPE_THETA = 10000.0
NORM_EPS = 1e-6

LANES = 128
NEG_BIG = -1e9

ROW_TILE = 256
MLA_TQ = 128
MLA_TK = 512
MOBA_TK = 512
MOE_ROWS = 256
MLA_PAGES = 16
MOBA_PAGES = 16
MOBA_HEAD_GROUP = 4
VMEM_LIMIT = 48 * 1024 * 1024

_bf16 = jnp.bfloat16
_f32 = jnp.float32


def _dot(a, b):
    return jnp.dot(a, b, preferred_element_type=_f32)


def _dot_nt(a, b):
    return lax.dot_general(a, b, (((1,), (1,)), ((), ())), preferred_element_type=_f32)


def _rms(x, g):
    return x * lax.rsqrt(jnp.mean(x * x, axis=-1, keepdims=True) + NORM_EPS) * g


def _rope_apply(x, cos, sin_lo, sin_hi, half):
    w = x.shape[-1]
    return x * cos + pltpu.roll(x, w - half, 1) * sin_lo + pltpu.roll(x, half, 1) * sin_hi


def _proj_kernel(x_ref, g_attn_ref, w_in_ref, g_q_ref, w_uqn_ref, w_uqr_ref, g_kv_ref, w_ukp_ref,
                 cb_ref, slb_ref, shb_ref, ca_ref, sla_ref, sha_ref,
                 ckv_ref, kpe_ref, kb_ref, vb_ref, qcat_ref, *rest, sample, tiles_per_seq):
    tm = x_ref.shape[0]
    h = _rms(x_ref[...], g_attn_ref[...])
    z = _dot(h.astype(_bf16), w_in_ref[...])
    c_q = z[:, 0:256]
    c_kv = z[:, 256:384]
    q_b = z[:, 384:896]
    k_b = z[:, 896:1408]
    v_b = z[:, 1408:1920]
    k_pe = z[:, 1920:2048]

    cqn = _rms(c_q, g_q_ref[...]).astype(_bf16)
    q_nope = _dot(cqn, w_uqn_ref[...])
    q_rope = _dot(cqn, w_uqr_ref[...])
    wide = lambda ref, reps: jnp.concatenate([ref[...]] * reps, axis=1)
    reps_a = MLA_HEADS * MLA_D_ROPE // LANES
    ca, sla, sha = wide(ca_ref, reps_a), wide(sla_ref, reps_a), wide(sha_ref, reps_a)
    q_pe = _rope_apply(q_rope, ca, sla, sha, MLA_D_ROPE // 2) * MLA_SCALE
    ckv = _rms(c_kv, g_kv_ref[...])
    kpe = _rope_apply(k_pe, ca[:, :LANES], sla[:, :LANES], sha[:, :LANES], MLA_D_ROPE // 2)[:, :MLA_D_ROPE]
    ckv_ref[...] = ckv
    kpe_ref[...] = kpe

    reps_b = MOBA_WIDTH // LANES
    cb, slb, shb = wide(cb_ref, reps_b), wide(slb_ref, reps_b), wide(shb_ref, reps_b)
    q_b = _rope_apply(q_b, cb, slb, shb, MOBA_HEAD_DIM // 2) * MOBA_SCALE
    k_b = _rope_apply(k_b, cb, slb, shb, MOBA_HEAD_DIM // 2)
    kb_ref[...] = k_b
    vb_ref[...] = v_b

    for p in range(MLA_HEADS // 2):
        qn = q_nope[:, p * LANES:(p + 1) * LANES].astype(_bf16)
        ql = _dot(qn, w_ukp_ref[p]) * MLA_SCALE
        for u in range(2):
            hd = 2 * p + u
            qcat_ref[hd, :, 0:MLA_KV_LORA] = ql[:, u * LANES:(u + 1) * LANES].astype(_bf16)
            qcat_ref[hd, :, MLA_KV_LORA:MLA_QK] = q_pe[:, hd * MLA_D_ROPE:(hd + 1) * MLA_D_ROPE].astype(_bf16)

    if sample:
        (qb_ref,) = rest
        qb_ref[...] = q_b
        return

    kcat_ref, qbh_ref, kaug_ref, vbh_ref, kmean_ref = rest
    kcat_ref[:, 0:MLA_KV_LORA] = ckv.astype(_bf16)
    kcat_ref[:, MLA_KV_LORA:MLA_QK] = kpe.astype(_bf16)
    kmean_ref[0] = jnp.sum(k_b, axis=0, keepdims=True) * (1.0 / MOBA_BLOCK)
    blk = pl.program_id(0) % tiles_per_seq
    lane = lax.broadcasted_iota(jnp.int32, (tm, LANES), 1)
    onehot = jnp.where(lane == MOBA_HEAD_DIM + blk, 1.0, 0.0)
    for p in range(MOBA_HEADS // 2):
        sl = slice(p * LANES, (p + 1) * LANES)
        qp, kp, vp = q_b[:, sl], k_b[:, sl], v_b[:, sl]
        kp_sw = pltpu.roll(kp, MOBA_HEAD_DIM, 1)
        for u, ksrc in ((0, kp), (1, kp_sw)):
            hd = 2 * p + u
            kaug_ref[hd] = jnp.where(lane < MOBA_HEAD_DIM, ksrc, onehot).astype(_bf16)
            qbh_ref[hd] = qp[:, u * MOBA_HEAD_DIM:(u + 1) * MOBA_HEAD_DIM].astype(_bf16)
            vbh_ref[hd] = vp[:, u * MOBA_HEAD_DIM:(u + 1) * MOBA_HEAD_DIM].astype(_bf16)


def _rope_tables(pos, d, reps):
    half = d // 2
    inv = jnp.power(ROPE_THETA, -jnp.arange(half, dtype=_f32) * (2.0 / d))
    ang = pos.astype(_f32)[:, None] * inv[None, :]
    cos, sin, zero = jnp.cos(ang), jnp.sin(ang), jnp.zeros_like(ang)
    tile = lambda a, b: jnp.tile(jnp.concatenate([a, b], axis=1), (1, reps))
    return tile(cos, cos), tile(-sin, zero), tile(zero, sin)


def _project(x, pos, wts, *, sample):
    n = x.shape[0]
    tm = ROW_TILE
    assert n % tm == 0 and pos.shape[0] % tm == 0
    tiles_per_seq = pos.shape[0] // tm
    tabs_b = _rope_tables(pos, MOBA_HEAD_DIM, LANES // MOBA_HEAD_DIM)
    tabs_a = _rope_tables(pos, MLA_D_ROPE, LANES // MLA_D_ROPE)
    row = lambda w: pl.BlockSpec((tm, w), lambda i: (i, 0))
    full = lambda a: pl.BlockSpec(a.shape, lambda i: (0,) * a.ndim)
    tab = lambda w: pl.BlockSpec((tm, w), lambda i: (i % tiles_per_seq, 0))
    heads = lambda w: pl.BlockSpec((MLA_HEADS, tm, w), lambda i: (0, i, 0))
    out_shape = [jax.ShapeDtypeStruct((n, MLA_KV_LORA), _f32), jax.ShapeDtypeStruct((n, MLA_D_ROPE), _f32),
                 jax.ShapeDtypeStruct((n, MOBA_WIDTH), _f32), jax.ShapeDtypeStruct((n, MOBA_WIDTH), _f32),
                 jax.ShapeDtypeStruct((MLA_HEADS, n, MLA_QK), _bf16)]
    out_specs = [row(MLA_KV_LORA), row(MLA_D_ROPE), row(MOBA_WIDTH), row(MOBA_WIDTH), heads(MLA_QK)]
    if sample:
        out_shape += [jax.ShapeDtypeStruct((n, MOBA_WIDTH), _f32)]
        out_specs += [row(MOBA_WIDTH)]
    else:
        out_shape += [jax.ShapeDtypeStruct((n, MLA_QK), _bf16),
                      jax.ShapeDtypeStruct((MOBA_HEADS, n, MOBA_HEAD_DIM), _bf16),
                      jax.ShapeDtypeStruct((MOBA_HEADS, n, LANES), _bf16),
                      jax.ShapeDtypeStruct((MOBA_HEADS, n, MOBA_HEAD_DIM), _bf16),
                      jax.ShapeDtypeStruct((n // tm, 1, MOBA_WIDTH), _f32)]
        out_specs += [row(MLA_QK), heads(MOBA_HEAD_DIM), heads(LANES), heads(MOBA_HEAD_DIM),
                      pl.BlockSpec((1, 1, MOBA_WIDTH), lambda i: (i, 0, 0))]
    w = wts
    ins = [x, w['g_attn'], w['w_in'], w['g_q'], w['w_uqn'], w['w_uqr'], w['g_kv'], w['w_ukp'], *tabs_b, *tabs_a]
    in_specs = [row(D_MODEL)] + [full(a) for a in ins[1:8]] + [tab(LANES)] * 6
    return pl.pallas_call(
        functools.partial(_proj_kernel, sample=sample, tiles_per_seq=tiles_per_seq),
        grid=(n // tm,), in_specs=in_specs, out_specs=out_specs, out_shape=out_shape,
        compiler_params=pltpu.CompilerParams(dimension_semantics=("parallel",), vmem_limit_bytes=VMEM_LIMIT),
    )(*ins)


def _mla_prompt_kernel(qi_ref, ki_ref, pt_ref, q_ref, k_ref, wuv_ref, qt_ref, kt_hbm, o_ref, idx_ref,
                       m_sc, l_sc, acc_sc, kbuf, ksum_sc, sem, *, tq, tk, sel):
    s_id = pl.program_id(1)
    qi, ki = qi_ref[s_id], ki_ref[s_id]
    rows = MLA_HEADS * tq
    last_k = (qi * tq) // tk

    g = pl.program_id(0) * pl.num_programs(1) + s_id

    @pl.when(g < sel['total'])
    def _():
        _select_step(g, sel['total'], pt_ref, qt_ref, kt_hbm, idx_ref, kbuf, ksum_sc, sem,
                     cp=sel['cp'], nch=sel['nch'], dt=sel['dt'], nb=sel['nb'])

    @pl.when(ki == 0)
    def _():
        m_sc[...] = jnp.full(m_sc.shape, -jnp.inf, _f32)
        l_sc[...] = jnp.zeros(l_sc.shape, _f32)
        acc_sc[...] = jnp.zeros(acc_sc.shape, _f32)

    q = q_ref[...].reshape(rows, MLA_QK)
    k = k_ref[0]
    s = _dot_nt(q, k)

    def update(s):
        m_prev = m_sc[...]
        m_new = jnp.maximum(m_prev, jnp.max(s, axis=-1, keepdims=True))
        alpha = jnp.exp(m_prev - m_new)
        p = jnp.exp(s - m_new)
        l_sc[...] = alpha * l_sc[...] + jnp.sum(p, axis=-1, keepdims=True)
        acc_sc[...] = alpha * acc_sc[...] + _dot(p.astype(_bf16), k[:, :MLA_KV_LORA])
        m_sc[...] = m_new

    @pl.when(ki < last_k)
    def _():
        update(s)

    @pl.when(ki == last_k)
    def _():
        qpos = qi * tq + lax.broadcasted_iota(jnp.int32, (rows, tk), 0) % tq
        kpos = ki * tk + lax.broadcasted_iota(jnp.int32, (rows, tk), 1)
        update(jnp.where(kpos <= qpos, s, -jnp.inf))
        o = acc_sc[...] / l_sc[...]
        for p in range(MLA_HEADS // 2):
            pair = jnp.concatenate([o[(2 * p) * tq:(2 * p + 1) * tq], o[(2 * p + 1) * tq:(2 * p + 2) * tq]], axis=1)
            o_ref[0, :, p * LANES:(p + 1) * LANES] = _dot(pair.astype(_bf16), wuv_ref[p]).astype(o_ref.dtype)


def _mla_prompt(qcat, kcat, w_uvp, bsz, seq, qt, kt_pool, page_table):
    tq, tk = min(MLA_TQ, seq), min(MLA_TK, seq)
    assert seq % tq == 0 and seq % tk == 0 and tk % tq == 0
    nq = seq // tq
    pairs = [(q, k) for q in range(nq) for k in range((q * tq) // tk + 1)]
    qi = jnp.asarray([p[0] for p in pairs], jnp.int32)
    ki = jnp.asarray([p[1] for p in pairs], jnp.int32)
    rows = MLA_HEADS * tq
    npairs = len(pairs)
    db, npg = page_table.shape
    dt = qt.shape[-1]
    cp, nch, nb = _select_plan(db, npg, bsz * npairs)
    sel = dict(total=db * nch, cp=cp, nch=nch, dt=dt, nb=nb)
    seq_of = lambda b, s: jnp.minimum((b * npairs + s) // nch, db - 1)
    grid_spec = pltpu.PrefetchScalarGridSpec(
        num_scalar_prefetch=3, grid=(bsz, npairs),
        in_specs=[pl.BlockSpec((MLA_HEADS, tq, MLA_QK), lambda b, s, qi, ki, pt: (0, b * nq + qi[s], 0)),
                  pl.BlockSpec((1, tk, MLA_QK), lambda b, s, qi, ki, pt: (b, ki[s], 0)),
                  pl.BlockSpec(w_uvp.shape, lambda b, s, qi, ki, pt: (0, 0, 0)),
                  pl.BlockSpec((1, MOBA_HEADS, MOBA_HEAD_DIM, dt), lambda b, s, qi, ki, pt: (seq_of(b, s), 0, 0, 0)),
                  pl.BlockSpec(memory_space=pl.ANY)],
        out_specs=[pl.BlockSpec((1, tq, MLA_HEADS * MLA_D_V), lambda b, s, qi, ki, pt: (b, qi[s], 0)),
                   pl.BlockSpec((1, MOBA_HEADS, dt, LANES), lambda b, s, qi, ki, pt: (seq_of(b, s), 0, 0, 0))],
        scratch_shapes=[pltpu.VMEM((rows, 1), _f32), pltpu.VMEM((rows, 1), _f32),
                        pltpu.VMEM((rows, MLA_KV_LORA), _f32),
                        pltpu.VMEM((2, cp, MOBA_HEADS, MOBA_HEAD_DIM, PAGE_SIZE), _f32),
                        pltpu.VMEM((MOBA_HEADS, MOBA_HEAD_DIM, LANES), _f32), pltpu.SemaphoreType.DMA((2,))])
    return pl.pallas_call(
        functools.partial(_mla_prompt_kernel, tq=tq, tk=tk, sel=sel), grid_spec=grid_spec,
        out_shape=[jax.ShapeDtypeStruct((bsz, seq, MLA_HEADS * MLA_D_V), _bf16),
                   jax.ShapeDtypeStruct((db, MOBA_HEADS, dt, LANES), jnp.int32)],
        compiler_params=pltpu.CompilerParams(dimension_semantics=("arbitrary", "arbitrary"),
                                             vmem_limit_bytes=VMEM_LIMIT),
    )(qi, ki, page_table, qcat, kcat.reshape(bsz, seq, MLA_QK), w_uvp, qt, kt_pool)


def _top_blocks(gate, lane, n_valid, first_lane):
    g = jnp.where((lane >= first_lane) & (lane < first_lane + n_valid), gate, -jnp.inf)
    lane_f = lane.astype(_f32)
    chosen = jnp.zeros(gate.shape, jnp.bool_)
    for _ in range(MOBA_TOPK):
        mx = jnp.max(g, axis=-1, keepdims=True)
        idx = jnp.min(jnp.where(g == mx, lane_f, 1e9), axis=-1, keepdims=True)
        pick = (lane_f == idx) & (mx > -jnp.inf)
        chosen = chosen | pick
        g = jnp.where(pick, -jnp.inf, g)
    return chosen


def _moba_prompt_kernel(qi_ref, kt_ref, q_ref, kaug_ref, v_ref, kmean_ref, o_ref, qaug_sc, m_sc, l_sc, acc_sc, *, tk):
    s_id = pl.program_id(1)
    j, kt = qi_ref[s_id], kt_ref[s_id]
    t = MOBA_BLOCK
    nh = MOBA_HEADS

    @pl.when(kt == 0)
    def _():
        lane = lax.broadcasted_iota(jnp.int32, (t, LANES), 1)
        for h in range(nh):
            q = q_ref[h]
            gate = _dot_nt(q, kmean_ref[0, h])
            chosen = _top_blocks(gate, lane, j, MOBA_HEAD_DIM) | (lane == MOBA_HEAD_DIM + j)
            bias = jnp.where(chosen, 0.0, NEG_BIG)
            q_wide = jnp.concatenate([q.astype(_f32), jnp.zeros((t, MOBA_HEAD_DIM), _f32)], axis=1)
            qaug_sc[h] = jnp.where(lane < MOBA_HEAD_DIM, q_wide, bias).astype(_bf16)
        m_sc[...] = jnp.full(m_sc.shape, -jnp.inf, _f32)
        l_sc[...] = jnp.zeros(l_sc.shape, _f32)
        acc_sc[...] = jnp.zeros(acc_sc.shape, _f32)

    def update(causal):
        for h in range(nh):
            s = _dot_nt(qaug_sc[h], kaug_ref[h])
            if causal is not None:
                s = jnp.where(causal, s, -jnp.inf)
            m_prev = m_sc[h]
            m_new = jnp.maximum(m_prev, jnp.max(s, axis=-1, keepdims=True))
            alpha = jnp.exp(m_prev - m_new)
            p = jnp.exp(s - m_new)
            l_sc[h] = alpha * l_sc[h] + jnp.sum(p, axis=-1, keepdims=True)
            acc_sc[h] = alpha * acc_sc[h] + _dot(p.astype(_bf16), v_ref[h])
            m_sc[h] = m_new

    last = (j * t) // tk

    @pl.when(kt < last)
    def _():
        update(None)

    @pl.when(kt == last)
    def _():
        qpos = j * t + lax.broadcasted_iota(jnp.int32, (t, tk), 0)
        kpos = kt * tk + lax.broadcasted_iota(jnp.int32, (t, tk), 1)
        update(kpos <= qpos)
        o_ref[0] = jnp.concatenate([acc_sc[h] / l_sc[h] for h in range(nh)], axis=1).astype(o_ref.dtype)


def _moba_prompt(qbh, kaug, vbh, kmean, bsz, seq):
    t = MOBA_BLOCK
    tk = min(MOBA_TK, seq)
    assert seq % tk == 0 and tk % t == 0
    nb = seq // t
    assert nb <= LANES - MOBA_HEAD_DIM
    nh = MOBA_HEADS
    km = kmean.reshape(bsz, nb, nh, MOBA_HEAD_DIM).transpose(0, 2, 1, 3)
    km = jnp.pad(km, ((0, 0), (0, 0), (MOBA_HEAD_DIM, LANES - MOBA_HEAD_DIM - nb), (0, 0))).astype(_bf16)
    pairs = [(q, k) for q in range(nb) for k in range((q * t) // tk + 1)]
    qi = jnp.asarray([p[0] for p in pairs], jnp.int32)
    kt = jnp.asarray([p[1] for p in pairs], jnp.int32)
    ntk = seq // tk
    qmap = lambda b, s, qi, kt: (0, b * nb + qi[s], 0)
    kmap = lambda b, s, qi, kt: (0, b * ntk + kt[s], 0)
    grid_spec = pltpu.PrefetchScalarGridSpec(
        num_scalar_prefetch=2, grid=(bsz, len(pairs)),
        in_specs=[pl.BlockSpec((nh, t, MOBA_HEAD_DIM), qmap),
                  pl.BlockSpec((nh, tk, LANES), kmap),
                  pl.BlockSpec((nh, tk, MOBA_HEAD_DIM), kmap),
                  pl.BlockSpec((1, nh, LANES, MOBA_HEAD_DIM), lambda b, s, qi, kt: (b, 0, 0, 0))],
        out_specs=pl.BlockSpec((1, t, MOBA_WIDTH), lambda b, s, qi, kt: (b, qi[s], 0)),
        scratch_shapes=[pltpu.VMEM((nh, t, LANES), _bf16), pltpu.VMEM((nh, t, 1), _f32),
                        pltpu.VMEM((nh, t, 1), _f32), pltpu.VMEM((nh, t, MOBA_HEAD_DIM), _f32)])
    return pl.pallas_call(
        functools.partial(_moba_prompt_kernel, tk=tk), grid_spec=grid_spec,
        out_shape=jax.ShapeDtypeStruct((bsz, seq, MOBA_WIDTH), _bf16),
        compiler_params=pltpu.CompilerParams(dimension_semantics=("parallel", "arbitrary"),
                                             vmem_limit_bytes=VMEM_LIMIT),
    )(qi, kt, qbh, kaug, vbh, km)


def _prep_attn_weights(g_attn, w_in, g_q, w_uq, g_kv, w_uk, w_uv):
    c0, c1, c2 = MLA_Q_LORA, MLA_Q_LORA + MLA_KV_LORA, MLA_Q_LORA + MLA_KV_LORA + MLA_D_ROPE
    w_in_r = jnp.concatenate([w_in[:, :c1], w_in[:, c2:], w_in[:, c1:c2],
                              jnp.zeros((D_MODEL, LANES - MLA_D_ROPE), w_in.dtype)], axis=1)
    w_uqn = w_uq[:, :, :MLA_D_NOPE].reshape(MLA_Q_LORA, MLA_HEADS * MLA_D_NOPE)
    w_uqr = w_uq[:, :, MLA_D_NOPE:].reshape(MLA_Q_LORA, MLA_HEADS * MLA_D_ROPE)
    ukt = jnp.transpose(w_uk, (1, 2, 0))
    zk = jnp.zeros_like(ukt[0])
    w_ukp = jnp.stack([jnp.concatenate([jnp.concatenate([ukt[2 * p], zk], axis=1),
                                        jnp.concatenate([zk, ukt[2 * p + 1]], axis=1)], axis=0)
                       for p in range(MLA_HEADS // 2)])
    uv = jnp.transpose(w_uv, (1, 0, 2))
    zv = jnp.zeros_like(uv[0])
    w_uvp = jnp.stack([jnp.concatenate([jnp.concatenate([uv[2 * p], zv], axis=1),
                                        jnp.concatenate([zv, uv[2 * p + 1]], axis=1)], axis=0)
                       for p in range(MLA_HEADS // 2)])
    return {'g_attn': g_attn.reshape(1, -1), 'w_in': w_in_r.astype(_bf16), 'g_q': g_q.reshape(1, -1),
            'w_uqn': w_uqn.astype(_bf16), 'w_uqr': w_uqr.astype(_bf16), 'g_kv': g_kv.reshape(1, -1),
            'w_ukp': w_ukp.astype(_bf16), 'w_uvp': w_uvp.astype(_bf16)}


def _prep_post_weights(w_out, g_ffn, w_gr, b_gr, w_er, b_er, g_ple, w_ple_gate, w_ple_proj, g_final):
    pad = LANES - MOE_EXPERTS - MOE_GROUPS
    w_r = jnp.concatenate([w_er, w_gr, jnp.zeros((D_MODEL, pad), w_er.dtype)], axis=1)
    b_r = jnp.concatenate([b_er, b_gr, jnp.zeros((pad,), b_er.dtype)]).reshape(1, LANES)
    w_r_hi = w_r.astype(_bf16)
    w_r_lo = (w_r - w_r_hi.astype(_f32)).astype(_bf16)
    return {'w_out': w_out.astype(_bf16), 'g_ffn': g_ffn.reshape(1, -1), 'w_r_hi': w_r_hi, 'w_r_lo': w_r_lo,
            'b_r': b_r, 'g_ple': g_ple.reshape(1, -1), 'w_ple_gate': w_ple_gate.astype(_bf16),
            'w_ple_proj': w_ple_proj.astype(_bf16), 'g_final': g_final.reshape(1, -1)}


def _route_kernel(x_ref, omla_ref, omoba_ref, cnt0_ref, w_out_ref, g_ffn_ref, wr_hi_ref, wr_lo_ref, br_ref,
                  x1_ref, h2_ref, ri_ref, rw_ref, cnt_ref, cnt_sc):
    tm = x_ref.shape[0]

    @pl.when(pl.program_id(0) == 0)
    def _():
        cnt_sc[...] = cnt0_ref[...]

    om = jnp.concatenate([omla_ref[...], omoba_ref[...]], axis=1)
    x1 = x_ref[...] + _dot(om, w_out_ref[...])
    x1_ref[...] = x1
    h2 = _rms(x1, g_ffn_ref[...])
    h2_ref[...] = h2
    hi = h2.astype(_bf16)
    lo = (h2 - hi.astype(_f32)).astype(_bf16)
    logits = _dot(hi, wr_hi_ref[...]) + _dot(lo, wr_hi_ref[...]) + _dot(hi, wr_lo_ref[...]) + br_ref[...]

    lane = lax.broadcasted_iota(jnp.int32, (tm, LANES), 1)
    big = jnp.int32(2 ** 30)
    first = lambda mask: jnp.min(jnp.where(mask, lane, big), axis=-1, keepdims=True)
    gl = jnp.where((lane >= MOE_EXPERTS) & (lane < MOE_EXPERTS + MOE_GROUPS), logits, -jnp.inf)
    gmax = jnp.max(gl, axis=-1, keepdims=True)
    g_w = 1.0 / jnp.sum(jnp.exp(gl - gmax), axis=-1, keepdims=True)
    g_idx = first(gl == gmax) - MOE_EXPERTS
    e0 = g_idx * MOE_EXPERTS_PER_GROUP
    el = jnp.where((lane >= e0) & (lane < e0 + MOE_EXPERTS_PER_GROUP), logits, -jnp.inf)
    emax = jnp.max(el, axis=-1, keepdims=True)
    esum = jnp.sum(jnp.exp(el - emax), axis=-1, keepdims=True)
    i1 = first(el == emax)
    el2 = jnp.where(lane == i1, -jnp.inf, el)
    e2max = jnp.max(el2, axis=-1, keepdims=True)
    i2 = first(el2 == e2max)
    p1 = 1.0 / esum
    p2 = jnp.exp(e2max - emax) / esum
    w1 = g_w * p1 / (p1 + p2)
    w2 = g_w * p2 / (p1 + p2)
    oh1, oh2 = lane == i1, lane == i2
    both = jnp.where(oh1 | oh2, 1.0, 0.0)
    rowi = lax.broadcasted_iota(jnp.int32, (tm, tm), 0)
    coli = lax.broadcasted_iota(jnp.int32, (tm, tm), 1)
    tri = jnp.where(coli < rowi, 1.0, 0.0).astype(_bf16)
    before = cnt_sc[...] + _dot(tri, both.astype(_bf16))
    r1 = jnp.sum(jnp.where(oh1, before, 0.0), axis=-1, keepdims=True).astype(jnp.int32)
    r2 = jnp.sum(jnp.where(oh2, before, 0.0), axis=-1, keepdims=True).astype(jnp.int32)
    cnt = cnt_sc[...] + jnp.sum(both, axis=0, keepdims=True)
    cnt_sc[...] = cnt
    cnt_ref[...] = cnt
    l8 = lax.broadcasted_iota(jnp.int32, (tm, 8), 1)
    ri_ref[...] = jnp.where(l8 == 0, i1, jnp.where(l8 == 1, i2, jnp.where(l8 == 2, r1, jnp.where(l8 == 3, r2, 0))))
    rw_ref[...] = jnp.where(l8 == 0, w1, jnp.where(l8 == 1, w2, 0.0))


def _route(x, o_mla, o_moba, cnt0, w):
    n = x.shape[0]
    tm = ROW_TILE
    row = lambda wd: pl.BlockSpec((tm, wd), lambda i: (i, 0))
    full = lambda a: pl.BlockSpec(a.shape, lambda i: (0,) * a.ndim)
    ws = [cnt0, w['w_out'], w['g_ffn'], w['w_r_hi'], w['w_r_lo'], w['b_r']]
    return pl.pallas_call(
        _route_kernel, grid=(n // tm,),
        in_specs=[row(D_MODEL), row(MLA_HEADS * MLA_D_V), row(MOBA_WIDTH)] + [full(a) for a in ws],
        out_specs=[row(D_MODEL), row(D_MODEL), row(8), row(8), pl.BlockSpec((1, LANES), lambda i: (0, 0))],
        out_shape=[jax.ShapeDtypeStruct((n, D_MODEL), _f32), jax.ShapeDtypeStruct((n, D_MODEL), _f32),
                   jax.ShapeDtypeStruct((n, 8), jnp.int32), jax.ShapeDtypeStruct((n, 8), _f32),
                   jax.ShapeDtypeStruct((1, LANES), _f32)],
        scratch_shapes=[pltpu.VMEM((1, LANES), _f32)],
        compiler_params=pltpu.CompilerParams(dimension_semantics=("arbitrary",), vmem_limit_bytes=VMEM_LIMIT),
    )(x, o_mla, o_moba, *ws)


def _scatter_kernel(dest_ref, h_ref, xs_in_ref, xs_ref, sem):
    del xs_in_ref
    tm = h_ref.shape[0]

    def start(r, c):
        for k in range(2):
            pltpu.make_async_copy(h_ref.at[pl.ds(r, 1)], xs_ref.at[pl.ds(dest_ref[2 * r + k], 1)], sem).start()
        return c

    lax.fori_loop(0, tm, start, 0, unroll=8)
    for _ in range(2 * tm):
        pltpu.make_async_copy(h_ref.at[pl.ds(0, 1)], xs_ref.at[pl.ds(0, 1)], sem).wait()


def _scatter_rows(h2, dest, xs):
    n = h2.shape[0]
    tm = ROW_TILE
    return pl.pallas_call(
        _scatter_kernel, grid=(n // tm,),
        in_specs=[pl.BlockSpec((2 * tm,), lambda i: (i,), memory_space=pltpu.SMEM),
                  pl.BlockSpec((tm, D_MODEL), lambda i: (i, 0)),
                  pl.BlockSpec(memory_space=pl.ANY)],
        out_specs=pl.BlockSpec(memory_space=pl.ANY),
        out_shape=jax.ShapeDtypeStruct(xs.shape, xs.dtype),
        scratch_shapes=[pltpu.SemaphoreType.DMA(())],
        input_output_aliases={2: 0},
        compiler_params=pltpu.CompilerParams(dimension_semantics=("arbitrary",), has_side_effects=True),
    )(dest.reshape(-1), h2, xs)


def _expert_kernel(be_ref, na_ref, xs_ref, wg_ref, wu_ref, wd_ref, yb_ref, wg_sc, wu_sc, wd_sc):
    b = pl.program_id(0)

    @pl.when(b < na_ref[0])
    def _():
        prev = be_ref[jnp.maximum(b - 1, 0)]

        @pl.when((b == 0) | (be_ref[b] != prev))
        def _():
            wg_sc[...] = wg_ref[0].astype(_bf16)
            wu_sc[...] = wu_ref[0].astype(_bf16)
            wd_sc[...] = wd_ref[0].astype(_bf16)

        xi = xs_ref[...].astype(_bf16)
        g = _dot(xi, wg_sc[...])
        u = _dot(xi, wu_sc[...])
        a = g * (1.0 / (1.0 + jnp.exp(-g))) * u
        yb_ref[...] = _dot(a.astype(_bf16), wd_sc[...])

    @pl.when(b >= na_ref[0])
    def _():
        yb_ref[...] = jnp.zeros(yb_ref.shape, yb_ref.dtype)


def _experts(xs, blk_exp, n_active, w_eg, w_eu, w_ed):
    rows = xs.shape[0]
    r = MOE_ROWS
    nblk = rows // r
    blk = lambda b, be, na: (jnp.minimum(b, na[0] - 1), 0)
    wmap = lambda b, be, na: (be[jnp.minimum(b, na[0] - 1)], 0, 0)
    grid_spec = pltpu.PrefetchScalarGridSpec(
        num_scalar_prefetch=2, grid=(nblk,),
        in_specs=[pl.BlockSpec((r, D_MODEL), blk),
                  pl.BlockSpec((1, D_MODEL, D_EXPERT), wmap), pl.BlockSpec((1, D_MODEL, D_EXPERT), wmap),
                  pl.BlockSpec((1, D_EXPERT, D_MODEL), wmap)],
        out_specs=pl.BlockSpec((r, D_MODEL), lambda b, be, na: (b, 0)),
        scratch_shapes=[pltpu.VMEM((D_MODEL, D_EXPERT), _bf16), pltpu.VMEM((D_MODEL, D_EXPERT), _bf16),
                        pltpu.VMEM((D_EXPERT, D_MODEL), _bf16)])
    return pl.pallas_call(
        _expert_kernel, grid_spec=grid_spec, out_shape=jax.ShapeDtypeStruct((rows, D_MODEL), _f32),
        compiler_params=pltpu.CompilerParams(dimension_semantics=("arbitrary",), vmem_limit_bytes=VMEM_LIMIT),
    )(blk_exp, n_active, xs, w_eg, w_eu, w_ed)


def _combine_kernel(dest_ref, dnext_ref, yb_ref, x1_ref, rw_ref, p_ref, g_ple_ref, w_pg_ref, w_pp_ref, g_fin_ref,
                    y_ref, ybuf, sem, *, final_norm):
    tm = x1_ref.shape[0]
    i = pl.program_id(0)
    slot = i % 2

    def gather(d_ref, sl):
        def start(r, c):
            for k in range(2):
                pltpu.make_async_copy(yb_ref.at[pl.ds(d_ref[2 * r + k], 1)], ybuf.at[sl, k, pl.ds(r, 1)],
                                      sem.at[sl]).start()
            return c
        lax.fori_loop(0, tm, start, 0, unroll=8)

    @pl.when(i == 0)
    def _():
        gather(dest_ref, slot)

    @pl.when(i + 1 < pl.num_programs(0))
    def _():
        gather(dnext_ref, 1 - slot)

    ple = _dot(p_ref[...].astype(_bf16), w_pp_ref[...])
    for _ in range(2 * tm):
        pltpu.make_async_copy(yb_ref.at[pl.ds(0, 1)], ybuf.at[slot, 0, pl.ds(0, 1)], sem.at[slot]).wait()
    rw = rw_ref[...]
    x2 = x1_ref[...] + (ybuf[slot, 0] * rw[:, 0:1] + ybuf[slot, 1] * rw[:, 1:2])
    gate = _dot(_rms(x2, g_ple_ref[...]).astype(_bf16), w_pg_ref[...])
    x3 = x2 + ple * (1.0 / (1.0 + jnp.exp(-gate)))
    y_ref[...] = _rms(x3, g_fin_ref[...]) if final_norm else x3


def _combine(yb, dest, x1, rw, p, w, final_norm):
    n = x1.shape[0]
    tm = ROW_TILE
    nt = n // tm
    row = lambda wd: pl.BlockSpec((tm, wd), lambda i: (i, 0))
    full = lambda a: pl.BlockSpec(a.shape, lambda i: (0,) * a.ndim)
    ws = [w['g_ple'], w['w_ple_gate'], w['w_ple_proj'], w['g_final']]
    dflat = dest.reshape(-1)
    return pl.pallas_call(
        functools.partial(_combine_kernel, final_norm=final_norm), grid=(nt,),
        in_specs=[pl.BlockSpec((2 * tm,), lambda i: (i,), memory_space=pltpu.SMEM),
                  pl.BlockSpec((2 * tm,), lambda i: (jnp.minimum(i + 1, nt - 1),), memory_space=pltpu.SMEM),
                  pl.BlockSpec(memory_space=pl.ANY), row(D_MODEL), row(8), row(PLE_DIM)] + [full(a) for a in ws],
        out_specs=row(D_MODEL),
        out_shape=jax.ShapeDtypeStruct((n, D_MODEL), _f32),
        scratch_shapes=[pltpu.VMEM((2, 2, tm, D_MODEL), _f32), pltpu.SemaphoreType.DMA((2,))],
        compiler_params=pltpu.CompilerParams(dimension_semantics=("arbitrary",), vmem_limit_bytes=VMEM_LIMIT),
    )(dflat, dflat, yb, x1, rw, p, *ws)


def _post(groups, w, w_eg, w_eu, w_ed, final_norm=True):
    r = MOE_ROWS
    cnt = jnp.zeros((1, LANES), _f32)
    routed = []
    for x, o_mla, o_moba, _ in groups:
        x1, h2, ri, rw, cnt = _route(x, o_mla, o_moba, cnt, w)
        routed.append((x1, h2, ri, rw))
    counts = cnt[0, :MOE_EXPERTS].astype(jnp.int32)
    padded = (counts + r - 1) // r * r
    pad_end = jnp.cumsum(padded)
    pad_start = pad_end - padded
    n_all = sum(g[0].shape[0] for g in groups)
    nblk = -(-(2 * n_all) // r) + MOE_EXPERTS
    first_row = jnp.arange(nblk, dtype=jnp.int32)[:, None] * r
    blk_exp = jnp.minimum(jnp.sum(pad_end[None, :] <= first_row, axis=1), MOE_EXPERTS - 1).astype(jnp.int32)
    n_active = (pad_end[-1:] // r).astype(jnp.int32)
    xs = jnp.zeros((nblk * r, D_MODEL), _f32)
    dests = []
    for x1, h2, ri, rw in routed:
        dest = pad_start[ri[:, 0:2]] + ri[:, 2:4]
        xs = _scatter_rows(h2, dest, xs)
        dests.append(dest)
    yb = _experts(xs, blk_exp, n_active, w_eg, w_eu, w_ed)
    return [_combine(yb, dest, x1, rw, g[3], w, final_norm) for (x1, h2, ri, rw), dest, g in zip(routed, dests, groups)]


def _paged_step(n_steps_inner):
    step = pl.program_id(0) * n_steps_inner + pl.program_id(1)
    return step, step % 2


def _run_paged(copies, step, slot, n_inner, total=None):
    if total is None:
        total = pl.num_programs(0) * n_inner

    @pl.when(step == 0)
    def _():
        for cp in copies(0, 0, slot, False):
            cp.start()

    @pl.when(step + 1 < total)
    def _():
        nxt = step + 1
        for cp in copies(nxt // n_inner, nxt % n_inner, 1 - slot, False):
            cp.start()

    for cp in copies(0, 0, slot, True):
        cp.wait()


def _mla_sample_kernel(pt_ref, q_ref, knew_ref, ckv_hbm, kpet_hbm, o_ref, cbuf, pbuf, m_sc, l_sc, acc_sc, sem,
                       *, cp, nch, dt):
    c = pl.program_id(1)
    step, slot = _paged_step(nch)

    def copies(bb, cc, sl, fixed):
        out = []
        for j in range(cp):
            page = 0 if fixed else pt_ref[bb, cc * cp + j]
            out.append(pltpu.make_async_copy(ckv_hbm.at[page], cbuf.at[sl, j], sem.at[0, sl]))
            out.append(pltpu.make_async_copy(kpet_hbm.at[page], pbuf.at[sl, j], sem.at[1, sl]))
        return out

    _run_paged(copies, step, slot, nch)

    @pl.when(c == 0)
    def _():
        m_sc[...] = jnp.full(m_sc.shape, -jnp.inf, _f32)
        l_sc[...] = jnp.zeros(l_sc.shape, _f32)
        acc_sc[...] = jnp.zeros(acc_sc.shape, _f32)

    def update(s, v):
        m_prev = m_sc[...]
        m_new = jnp.maximum(m_prev, jnp.max(s, axis=-1, keepdims=True))
        alpha = jnp.exp(m_prev - m_new)
        p = jnp.exp(s - m_new)
        l_sc[...] = alpha * l_sc[...] + jnp.sum(p, axis=-1, keepdims=True)
        acc_sc[...] = alpha * acc_sc[...] + _dot(p.astype(_bf16), v)
        m_sc[...] = m_new

    q = q_ref[0]
    q_lat, q_pe = q[:, :MLA_KV_LORA], q[:, MLA_KV_LORA:]
    kc = cbuf[slot].reshape(cp * PAGE_SIZE, MLA_KV_LORA).astype(_bf16)
    s_pe = jnp.concatenate([_dot(q_pe, pbuf[slot, j].astype(_bf16)) for j in range(cp)], axis=1)
    update(_dot_nt(q_lat, kc) + s_pe, kc)

    @pl.when(c == nch - 1)
    def _():
        knew = knew_ref[0]
        s_new = _dot_nt(q, knew)
        t_row = lax.broadcasted_iota(jnp.int32, s_new.shape, 0) % dt
        col = lax.broadcasted_iota(jnp.int32, s_new.shape, 1)
        update(jnp.where(col <= t_row, s_new, -jnp.inf), knew[:, :MLA_KV_LORA])
        o_ref[0] = acc_sc[...] / l_sc[...]


def _mla_sample(q_rows, k_new, ckv_pool, kpet_pool, page_table, dt):
    db, npg = page_table.shape
    cp = min(MLA_PAGES, npg)
    assert npg % cp == 0
    nch = npg // cp
    rows = MLA_HEADS * dt
    grid_spec = pltpu.PrefetchScalarGridSpec(
        num_scalar_prefetch=1, grid=(db, nch),
        in_specs=[pl.BlockSpec((1, rows, MLA_QK), lambda b, c, pt: (b, 0, 0)),
                  pl.BlockSpec((1,) + k_new.shape[1:], lambda b, c, pt: (b, 0, 0)),
                  pl.BlockSpec(memory_space=pl.ANY), pl.BlockSpec(memory_space=pl.ANY)],
        out_specs=pl.BlockSpec((1, rows, MLA_KV_LORA), lambda b, c, pt: (b, 0, 0)),
        scratch_shapes=[pltpu.VMEM((2, cp, PAGE_SIZE, MLA_KV_LORA), _f32),
                        pltpu.VMEM((2, cp, MLA_D_ROPE, PAGE_SIZE), _f32),
                        pltpu.VMEM((rows, 1), _f32), pltpu.VMEM((rows, 1), _f32),
                        pltpu.VMEM((rows, MLA_KV_LORA), _f32), pltpu.SemaphoreType.DMA((2, 2))])
    return pl.pallas_call(
        functools.partial(_mla_sample_kernel, cp=cp, nch=nch, dt=dt), grid_spec=grid_spec,
        out_shape=jax.ShapeDtypeStruct((db, rows, MLA_KV_LORA), _f32),
        compiler_params=pltpu.CompilerParams(dimension_semantics=("arbitrary", "arbitrary"),
                                             vmem_limit_bytes=VMEM_LIMIT),
    )(page_table, q_rows, k_new, ckv_pool, kpet_pool)


def _uv_kernel(o_ref, w_ref, out_ref):
    for p in range(MLA_HEADS // 2):
        pair = o_ref[:, 2 * p * MLA_KV_LORA:(2 * p + 2) * MLA_KV_LORA].astype(_bf16)
        out_ref[:, p * LANES:(p + 1) * LANES] = _dot(pair, w_ref[p]).astype(out_ref.dtype)


def _uv_project(o_tok, w_uvp):
    n = o_tok.shape[0]
    tm = ROW_TILE
    return pl.pallas_call(
        _uv_kernel, grid=(n // tm,),
        in_specs=[pl.BlockSpec((tm, MLA_HEADS * MLA_KV_LORA), lambda i: (i, 0)),
                  pl.BlockSpec(w_uvp.shape, lambda i: (0, 0, 0))],
        out_specs=pl.BlockSpec((tm, MLA_HEADS * MLA_D_V), lambda i: (i, 0)),
        out_shape=jax.ShapeDtypeStruct((n, MLA_HEADS * MLA_D_V), _bf16),
    )(o_tok, w_uvp)


def _top_idx(gate, lane, n_valid):
    g = jnp.where(lane < n_valid, gate, -jnp.inf)
    out = []
    for _ in range(MOBA_TOPK):
        mx = jnp.max(g, axis=-1, keepdims=True)
        idx = jnp.min(jnp.where(g == mx, lane, jnp.int32(2 ** 30)), axis=-1, keepdims=True)
        out.append(idx)
        g = jnp.where(lane == idx, -jnp.inf, g)
    return out


def _select_plan(db, npg, steps_available):
    ppb = MOBA_BLOCK // PAGE_SIZE
    nb = npg // ppb
    assert npg % ppb == 0 and MOBA_TOPK <= nb <= LANES
    cp = min(MOBA_PAGES, npg)
    while npg % cp or cp % ppb or db * (npg // cp) > steps_available:
        cp += ppb
        assert cp <= npg, "not enough host-kernel steps to stream the MoBA key cache"
    return cp, npg // cp, nb


def _select_step(step, total, pt_ref, qt_ref, kt_hbm, idx_ref, kbuf, ksum_sc, sem, *, cp, nch, dt, nb):
    c = step % nch
    slot = step % 2
    ppb = MOBA_BLOCK // PAGE_SIZE

    def copies(bb, cc, sl, fixed):
        return [pltpu.make_async_copy(kt_hbm.at[0 if fixed else pt_ref[bb, cc * cp + j]], kbuf.at[sl, j], sem.at[sl])
                for j in range(cp)]

    _run_paged(copies, step, slot, nch, total)

    @pl.when(c == 0)
    def _():
        ksum_sc[...] = jnp.zeros(ksum_sc.shape, _f32)

    lane = lax.broadcasted_iota(jnp.int32, (MOBA_HEAD_DIM, LANES), 1)
    for h in range(MOBA_HEADS):
        acc = ksum_sc[h]
        for jb in range(cp // ppb):
            blk = kbuf[slot, ppb * jb, h]
            for pg in range(1, ppb):
                blk = blk + kbuf[slot, ppb * jb + pg, h]
            red = jnp.sum(blk, axis=-1, keepdims=True)
            acc = jnp.where(lane == c * (cp // ppb) + jb, red, acc)
        ksum_sc[h] = acc

    @pl.when(c == nch - 1)
    def _():
        lane_t = lax.broadcasted_iota(jnp.int32, (dt, LANES), 1)
        for h in range(MOBA_HEADS):
            km = ksum_sc[h] * (1.0 / MOBA_BLOCK)
            qt = qt_ref[0, h]
            gate = jnp.concatenate([jnp.sum(km * qt[:, t:t + 1], axis=0, keepdims=True) for t in range(dt)], axis=0)
            i0, i1, i2 = _top_idx(gate, lane_t, nb)
            idx_ref[0, h] = jnp.where(lane_t == 0, i0, jnp.where(lane_t == 1, i1, i2))


def _moba_sample_kernel(pt_ref, sel_ref, qt_ref, knt_ref, vnt_ref, kt_hbm, vt_hbm, o_ref, kbuf, vbuf, sem,
                        *, hg, nhg, dt):
    g = pl.program_id(1)
    step, slot = _paged_step(nhg)
    ppb = MOBA_BLOCK // PAGE_SIZE
    per_q = MOBA_TOPK * ppb

    def copies(bb, gg, sl, fixed):
        out = []
        for hl in range(hg):
            h = 0 if fixed else gg * hg + hl
            for t in range(dt):
                for k in range(MOBA_TOPK):
                    blk = 0 if fixed else sel_ref[((bb * MOBA_HEADS + h) * dt + t) * MOBA_TOPK + k]
                    for pg in range(ppb):
                        page = 0 if fixed else pt_ref[bb, blk * ppb + pg]
                        i = (hl * dt + t) * per_q + k * ppb + pg
                        out.append(pltpu.make_async_copy(kt_hbm.at[page, h], kbuf.at[sl, i], sem.at[0, sl]))
                        out.append(pltpu.make_async_copy(vt_hbm.at[page, h], vbuf.at[sl, i], sem.at[1, sl]))
        return out

    _run_paged(copies, step, slot, nhg)

    lane_t = lax.broadcasted_iota(jnp.int32, (1, dt), 1)
    lane_o = lax.broadcasted_iota(jnp.int32, (MOBA_HEAD_DIM, dt), 1)
    for hl in range(hg):
        qt, knt, vnt = qt_ref[0, hl], knt_ref[0, hl], vnt_ref[0, hl]
        o_cols = jnp.zeros((MOBA_HEAD_DIM, dt), _f32)
        for t in range(dt):
            qcol = qt[:, t:t + 1]
            base = (hl * dt + t) * per_q
            s_sel = [jnp.sum(kbuf[slot, base + r] * qcol, axis=0, keepdims=True) for r in range(per_q)]
            s_own = jnp.where(lane_t <= t, jnp.sum(knt * qcol, axis=0, keepdims=True), -jnp.inf)
            m_row = s_sel[0]
            for r in range(1, per_q):
                m_row = jnp.maximum(m_row, s_sel[r])
            m = jnp.maximum(jnp.max(m_row, axis=-1, keepdims=True), jnp.max(s_own, axis=-1, keepdims=True))
            e_sel = [jnp.exp(s - m) for s in s_sel]
            e_own = jnp.exp(s_own - m)
            e_row = e_sel[0]
            acc = vbuf[slot, base] * e_sel[0]
            for r in range(1, per_q):
                e_row = e_row + e_sel[r]
                acc = acc + vbuf[slot, base + r] * e_sel[r]
            l = jnp.sum(e_row, axis=-1, keepdims=True) + jnp.sum(e_own, axis=-1, keepdims=True)
            o_col = (jnp.sum(acc, axis=-1, keepdims=True) + jnp.sum(vnt * e_own, axis=-1, keepdims=True)) / l
            o_cols = jnp.where(lane_o == t, o_col, o_cols)
        o_ref[0, hl] = o_cols


def _moba_sample(qt, knt, vnt, sel, kt_pool, vt_pool, page_table, dt):
    db, npg = page_table.shape
    hg = MOBA_HEAD_GROUP
    nhg = MOBA_HEADS // hg
    nslab = hg * dt * MOBA_TOPK * (MOBA_BLOCK // PAGE_SIZE)
    col = pl.BlockSpec((1, hg, MOBA_HEAD_DIM, dt), lambda b, g, pt, sel: (b, g, 0, 0))
    grid_spec = pltpu.PrefetchScalarGridSpec(
        num_scalar_prefetch=2, grid=(db, nhg),
        in_specs=[col, col, col, pl.BlockSpec(memory_space=pl.ANY), pl.BlockSpec(memory_space=pl.ANY)],
        out_specs=col,
        scratch_shapes=[pltpu.VMEM((2, nslab, MOBA_HEAD_DIM, PAGE_SIZE), _f32),
                        pltpu.VMEM((2, nslab, MOBA_HEAD_DIM, PAGE_SIZE), _f32), pltpu.SemaphoreType.DMA((2, 2))])
    return pl.pallas_call(
        functools.partial(_moba_sample_kernel, hg=hg, nhg=nhg, dt=dt), grid_spec=grid_spec,
        out_shape=jax.ShapeDtypeStruct((db, MOBA_HEADS, MOBA_HEAD_DIM, dt), _f32),
        compiler_params=pltpu.CompilerParams(dimension_semantics=("arbitrary", "arbitrary"),
                                             vmem_limit_bytes=VMEM_LIMIT),
    )(page_table, sel, qt, knt, vnt, kt_pool, vt_pool)


def _head_cols(a, db, dt):
    return a.reshape(db, dt, MOBA_HEADS, MOBA_HEAD_DIM).transpose(0, 2, 3, 1)


def _sample_attention(qcat_s, ckv_s, kpe_s, qt, kb_s, vb_s, sel, cache_ckv, kpet_pool, kt_pool, vt_pool, page_table,
                      w_uvp, db, dt):
    n_s = db * dt
    q_rows = qcat_s.reshape(MLA_HEADS, db, dt, MLA_QK).transpose(1, 0, 2, 3).reshape(db, MLA_HEADS * dt, MLA_QK)
    k_new = jnp.concatenate([ckv_s, kpe_s], axis=1).astype(_bf16).reshape(db, dt, MLA_QK)
    assert dt <= 16
    k_new = jnp.pad(k_new, ((0, 0), (0, 16 - dt), (0, 0)))
    o_lat = _mla_sample(q_rows, k_new, cache_ckv, kpet_pool, page_table, dt)
    o_tok = o_lat.reshape(db, MLA_HEADS, dt, MLA_KV_LORA).transpose(0, 2, 1, 3).reshape(n_s, -1)
    o_mla = _uv_project(o_tok, w_uvp)
    knt, vnt = _head_cols(kb_s, db, dt), _head_cols(vb_s, db, dt)
    ot = _moba_sample(qt, knt, vnt, sel[..., :MOBA_TOPK].reshape(-1), kt_pool, vt_pool, page_table, dt)
    o_moba = ot.transpose(0, 3, 1, 2).reshape(n_s, MOBA_WIDTH).astype(_bf16)
    return o_mla, o_moba


def kernel(x_prompt, x_sample, cache_ckv, cache_kpe, cache_k, cache_v, page_table, p_prompt, p_sample, g_attn, w_in,
           g_q, w_uq, g_kv, w_uk, w_uv, w_out, g_ffn, w_group_router, b_group_router, w_expert_router,
           b_expert_router, w_exp_gate, w_exp_up, w_exp_down, g_ple, w_ple_gate, w_ple_proj, g_final):
    bsz, seq, _ = x_prompt.shape
    db, dt, _ = x_sample.shape
    depth = g_attn.shape[0]
    n_p, n_s = bsz * seq, db * dt
    past = page_table.shape[1] * PAGE_SIZE
    assert ROW_TILE % dt == 0 and n_s % ROW_TILE == 0
    pos_p = jnp.arange(seq, dtype=jnp.int32)
    pos_s = past + jnp.arange(ROW_TILE, dtype=jnp.int32) % dt
    xp = x_prompt.reshape(n_p, D_MODEL)
    xs = x_sample.reshape(n_s, D_MODEL)
    outs_p, outs_s = [], []
    for i in range(depth):
        wa = _prep_attn_weights(g_attn[i], w_in[i], g_q[i], w_uq[i], g_kv[i], w_uk[i], w_uv[i])
        wp = _prep_post_weights(w_out[i], g_ffn[i], w_group_router[i], b_group_router[i], w_expert_router[i],
                                b_expert_router[i], g_ple[i], w_ple_gate[i], w_ple_proj[i], g_final)
        kpet_pool = jnp.transpose(cache_kpe[i], (0, 2, 1))
        kt_pool = jnp.transpose(cache_k[i], (0, 2, 3, 1))
        vt_pool = jnp.transpose(cache_v[i], (0, 2, 3, 1))
        ckv, kpe, kb, vb, qcat, kcat, qbh, kaug, vbh, kmean = _project(xp, pos_p, wa, sample=False)
        ckv_s, kpe_s, kb_s, vb_s, qcat_s, qb_s = _project(xs, pos_s, wa, sample=True)
        qt = _head_cols(qb_s, db, dt)
        o_mla, sel = _mla_prompt(qcat, kcat, wa['w_uvp'], bsz, seq, qt, kt_pool, page_table)
        o_mla = o_mla.reshape(n_p, MLA_HEADS * MLA_D_V)
        o_moba = _moba_prompt(qbh, kaug, vbh, kmean, bsz, seq).reshape(n_p, MOBA_WIDTH)
        outs_p.append((ckv, kpe, kb, vb))
        o_mla_s, o_moba_s = _sample_attention(qcat_s, ckv_s, kpe_s, qt, kb_s, vb_s, sel, cache_ckv[i], kpet_pool,
                                              kt_pool, vt_pool, page_table, wa['w_uvp'], db, dt)
        outs_s.append((ckv_s, kpe_s, kb_s, vb_s))
        xp, xs = _post([(xp, o_mla, o_moba, p_prompt[i].reshape(n_p, PLE_DIM)),
                        (xs, o_mla_s, o_moba_s, p_sample[i].reshape(n_s, PLE_DIM))],
                       wp, w_exp_gate[i], w_exp_up[i], w_exp_down[i], final_norm=(i == depth - 1))
    stack = lambda outs, j, shp: jnp.stack([o[j].reshape(shp) for o in outs])
    hs = (MOBA_HEADS, MOBA_HEAD_DIM)
    return (xp.reshape(bsz, seq, D_MODEL), xs.reshape(db, dt, D_MODEL),
            stack(outs_p, 0, (bsz, seq, MLA_KV_LORA)), stack(outs_p, 1, (bsz, seq, MLA_D_ROPE)),
            stack(outs_p, 2, (bsz, seq) + hs), stack(outs_p, 3, (bsz, seq) + hs),
            stack(outs_s, 0, (db, dt, MLA_KV_LORA)), stack(outs_s, 1, (db, dt, MLA_D_ROPE)),
            stack(outs_s, 2, (db, dt) + hs), stack(outs_s, 3, (db, dt) + hs))
```

```python
import functools

import jax
import jax.numpy as jnp
import numpy as np
from jax import lax
from jax.experimental import pallas as pl
from jax.experimental.pallas import tpu as pltpu

D_MODEL = 1024
PAGE_SIZE = 128
MLA_HEADS = 8
MLA_Q_LORA = 256
MLA_KV_LORA = 128
MLA_D_NOPE = 64
MLA_D_ROPE = 32
MLA_D_V = 64
MLA_SCALE = (MLA_D_NOPE + MLA_D_ROPE) ** -0.5
MLA_QK = MLA_KV_LORA + MLA_D_ROPE
MOBA_HEADS = 8
MOBA_HEAD_DIM = 64
MOBA_WIDTH = MOBA_HEADS * MOBA_HEAD_DIM
MOBA_BLOCK = 256
MOBA_TOPK = 3
MOBA_SCALE = MOBA_HEAD_DIM ** -0.5
MOE_GROUPS = 4
MOE_EXPERTS_PER_GROUP = 8
MOE_EXPERTS = MOE_GROUPS * MOE_EXPERTS_PER_GROUP
D_EXPERT = 512
PLE_DIM = 256
ROPE_THETA = 10000.0
NORM_EPS = 1e-6
LOG2E = 1.4426950408889634

LANES = 128
NEG_BIG = -1e9

ROW_TILE = 256
MLA_TQ = 128
MLA_TK = 512
MOBA_TK = 1024
MOE_ROWS = 256
MLA_PAGES = 32
MOBA_PAGES = 16
MOBA_HEAD_GROUP = 4
VMEM_LIMIT = 48 * 1024 * 1024

_bf16 = jnp.bfloat16
_f32 = jnp.float32


def _dot(a, b):
    return jnp.dot(a, b, preferred_element_type=_f32)


def _dot_nt(a, b):
    return lax.dot_general(a, b, (((1,), (1,)), ((), ())), preferred_element_type=_f32)


def _rms(x, g):
    return x * lax.rsqrt(jnp.mean(x * x, axis=-1, keepdims=True) + NORM_EPS) * g


def _rope_apply(x, cos, sin_lo, sin_hi, half):
    w = x.shape[-1]
    return x * cos + pltpu.roll(x, w - half, 1) * sin_lo + pltpu.roll(x, half, 1) * sin_hi


def _proj_kernel(x_ref, g_attn_ref, w_in_ref, g_q_ref, w_uqn_ref, w_uqr_ref, g_kv_ref, w_ukp_ref,
                 cb_ref, slb_ref, shb_ref, ca_ref, sla_ref, sha_ref,
                 ckv_ref, kpe_ref, kb_ref, vb_ref, qcat_ref, *rest, sample, tiles_per_seq):
    tm = x_ref.shape[0]
    h = _rms(x_ref[...], g_attn_ref[...])
    z = _dot(h.astype(_bf16), w_in_ref[...])
    c_q = z[:, 0:256]
    c_kv = z[:, 256:384]
    q_b = z[:, 384:896]
    k_b = z[:, 896:1408]
    v_b = z[:, 1408:1920]
    k_pe = z[:, 1920:2048]

    cqn = _rms(c_q, g_q_ref[...]).astype(_bf16)
    q_nope = _dot(cqn, w_uqn_ref[...])
    q_rope = _dot(cqn, w_uqr_ref[...])
    wide = lambda ref, reps: jnp.concatenate([ref[...]] * reps, axis=1)
    reps_a = MLA_HEADS * MLA_D_ROPE // LANES
    ca, sla, sha = wide(ca_ref, reps_a), wide(sla_ref, reps_a), wide(sha_ref, reps_a)
    base2 = 1.0 if sample else LOG2E
    q_pe = _rope_apply(q_rope, ca, sla, sha, MLA_D_ROPE // 2) * (MLA_SCALE * base2)
    ckv = _rms(c_kv, g_kv_ref[...])
    kpe = _rope_apply(k_pe, ca[:, :LANES], sla[:, :LANES], sha[:, :LANES], MLA_D_ROPE // 2)[:, :MLA_D_ROPE]
    ckv_ref[...] = ckv
    kpe_ref[...] = kpe

    reps_b = MOBA_WIDTH // LANES
    cb, slb, shb = wide(cb_ref, reps_b), wide(slb_ref, reps_b), wide(shb_ref, reps_b)
    q_b = _rope_apply(q_b, cb, slb, shb, MOBA_HEAD_DIM // 2) * (MOBA_SCALE * base2)
    k_b = _rope_apply(k_b, cb, slb, shb, MOBA_HEAD_DIM // 2)
    kb_ref[...] = k_b
    vb_ref[...] = v_b

    for p in range(MLA_HEADS // 2):
        qn = q_nope[:, p * LANES:(p + 1) * LANES].astype(_bf16)
        ql = _dot(qn, w_ukp_ref[p]) * (MLA_SCALE * base2)
        for u in range(2):
            hd = 2 * p + u
            qcat_ref[hd, :, 0:MLA_KV_LORA] = ql[:, u * LANES:(u + 1) * LANES].astype(_bf16)
            qcat_ref[hd, :, MLA_KV_LORA:MLA_QK] = q_pe[:, hd * MLA_D_ROPE:(hd + 1) * MLA_D_ROPE].astype(_bf16)

    if sample:
        (qb_ref,) = rest
        qb_ref[...] = q_b
        return

    kcat_ref, qbh_ref, kaug_ref, vbh_ref, kmean_ref = rest
    kcat_ref[:, 0:MLA_KV_LORA] = ckv.astype(_bf16)
    kcat_ref[:, MLA_KV_LORA:MLA_QK] = kpe.astype(_bf16)
    kmean_ref[0] = jnp.sum(k_b, axis=0, keepdims=True) * (1.0 / MOBA_BLOCK)
    blk = pl.program_id(0) % tiles_per_seq
    lane = lax.broadcasted_iota(jnp.int32, (tm, LANES), 1)
    onehot = jnp.where(lane == MOBA_HEAD_DIM + blk, 1.0, 0.0)
    ones_col = jnp.where(lane == MOBA_HEAD_DIM, 1.0, 0.0)
    for p in range(MOBA_HEADS // 2):
        sl = slice(p * LANES, (p + 1) * LANES)
        qp, kp, vp = q_b[:, sl], k_b[:, sl], v_b[:, sl]
        kp_sw = pltpu.roll(kp, MOBA_HEAD_DIM, 1)
        vp_sw = pltpu.roll(vp, MOBA_HEAD_DIM, 1)
        for u, ksrc, vsrc in ((0, kp, vp), (1, kp_sw, vp_sw)):
            hd = 2 * p + u
            kaug_ref[hd] = jnp.where(lane < MOBA_HEAD_DIM, ksrc, onehot).astype(_bf16)
            vbh_ref[hd] = jnp.where(lane < MOBA_HEAD_DIM, vsrc, ones_col).astype(_bf16)
            qbh_ref[hd] = qp[:, u * MOBA_HEAD_DIM:(u + 1) * MOBA_HEAD_DIM].astype(_bf16)


def _rope_tables(pos, d, reps):
    half = d // 2
    inv = jnp.power(ROPE_THETA, -jnp.arange(half, dtype=_f32) * (2.0 / d))
    ang = pos.astype(_f32)[:, None] * inv[None, :]
    cos, sin, zero = jnp.cos(ang), jnp.sin(ang), jnp.zeros_like(ang)
    tile = lambda a, b: jnp.tile(jnp.concatenate([a, b], axis=1), (1, reps))
    return tile(cos, cos), tile(-sin, zero), tile(zero, sin)


def _project(x, pos, wts, *, sample):
    n = x.shape[0]
    tm = ROW_TILE
    assert n % tm == 0 and pos.shape[0] % tm == 0
    tiles_per_seq = pos.shape[0] // tm
    tabs_b = _rope_tables(pos, MOBA_HEAD_DIM, LANES // MOBA_HEAD_DIM)
    tabs_a = _rope_tables(pos, MLA_D_ROPE, LANES // MLA_D_ROPE)
    row = lambda w: pl.BlockSpec((tm, w), lambda i: (i, 0))
    full = lambda a: pl.BlockSpec(a.shape, lambda i: (0,) * a.ndim)
    tab = lambda w: pl.BlockSpec((tm, w), lambda i: (i % tiles_per_seq, 0))
    heads = lambda w: pl.BlockSpec((MLA_HEADS, tm, w), lambda i: (0, i, 0))
    out_shape = [jax.ShapeDtypeStruct((n, MLA_KV_LORA), _f32), jax.ShapeDtypeStruct((n, MLA_D_ROPE), _f32),
                 jax.ShapeDtypeStruct((n, MOBA_WIDTH), _f32), jax.ShapeDtypeStruct((n, MOBA_WIDTH), _f32),
                 jax.ShapeDtypeStruct((MLA_HEADS, n, MLA_QK), _bf16)]
    out_specs = [row(MLA_KV_LORA), row(MLA_D_ROPE), row(MOBA_WIDTH), row(MOBA_WIDTH), heads(MLA_QK)]
    if sample:
        out_shape += [jax.ShapeDtypeStruct((n, MOBA_WIDTH), _f32)]
        out_specs += [row(MOBA_WIDTH)]
    else:
        out_shape += [jax.ShapeDtypeStruct((n, MLA_QK), _bf16),
                      jax.ShapeDtypeStruct((MOBA_HEADS, n, MOBA_HEAD_DIM), _bf16),
                      jax.ShapeDtypeStruct((MOBA_HEADS, n, LANES), _bf16),
                      jax.ShapeDtypeStruct((MOBA_HEADS, n, LANES), _bf16),
                      jax.ShapeDtypeStruct((n // tm, 1, MOBA_WIDTH), _f32)]
        out_specs += [row(MLA_QK), heads(MOBA_HEAD_DIM), heads(LANES), heads(LANES),
                      pl.BlockSpec((1, 1, MOBA_WIDTH), lambda i: (i, 0, 0))]
    w = wts
    ins = [x, w['g_attn'], w['w_in'], w['g_q'], w['w_uqn'], w['w_uqr'], w['g_kv'], w['w_ukp'], *tabs_b, *tabs_a]
    in_specs = [row(D_MODEL)] + [full(a) for a in ins[1:8]] + [tab(LANES)] * 6
    return pl.pallas_call(
        functools.partial(_proj_kernel, sample=sample, tiles_per_seq=tiles_per_seq),
        grid=(n // tm,), in_specs=in_specs, out_specs=out_specs, out_shape=out_shape,
        compiler_params=pltpu.CompilerParams(dimension_semantics=("parallel",), vmem_limit_bytes=VMEM_LIMIT),
    )(*ins)


def _mla_prompt_kernel(qi_ref, ki_ref, pt_ref, q_ref, k_ref, wuv_ref, qt_ref, kt_hbm, o_ref, idx_ref,
                       m_sc, l_sc, acc_sc, kbuf, ksum_sc, sem, *, tq, tk, sel):
    s_id = pl.program_id(1)
    qi, ki = qi_ref[s_id], ki_ref[s_id]
    rows = MLA_HEADS * tq
    last_k = (qi * tq) // tk

    g = pl.program_id(0) * pl.num_programs(1) + s_id

    @pl.when(g < sel['total'])
    def _():
        _select_fetch(g, sel['total'], pt_ref, kt_hbm, kbuf, sem, cp=sel['cp'], nch=sel['nch'])

    @pl.when(ki == 0)
    def _():
        m_sc[...] = jnp.full(m_sc.shape, -jnp.inf, _f32)
        l_sc[...] = jnp.zeros(l_sc.shape, _f32)
        acc_sc[...] = jnp.zeros(acc_sc.shape, _f32)

    def update(causal):
        _select_accumulate(g, kbuf, ksum_sc, cp=sel['cp'], nch=sel['nch'])
        q = q_ref[...].reshape(rows, MLA_QK)
        k = k_ref[0]
        s = _dot_nt(q, k)
        if causal is not None:
            s = jnp.where(causal, s, -jnp.inf)
        m_prev = m_sc[...]
        m_new = jnp.maximum(m_prev, jnp.max(s, axis=-1, keepdims=True))
        alpha = jnp.exp2(m_prev - m_new)
        p = jnp.exp2(s - m_new)
        l_sc[...] = alpha * l_sc[...] + jnp.sum(p, axis=-1, keepdims=True)
        acc_sc[...] = alpha * acc_sc[...] + _dot(p.astype(_bf16), k[:, :MLA_KV_LORA])
        m_sc[...] = m_new

    @pl.when(ki < last_k)
    def _():
        update(None)

    @pl.when(ki == last_k)
    def _():
        qpos = qi * tq + lax.broadcasted_iota(jnp.int32, (rows, tk), 0) % tq
        kpos = ki * tk + lax.broadcasted_iota(jnp.int32, (rows, tk), 1)
        update(kpos <= qpos)
        o = acc_sc[...] / l_sc[...]
        for p in range(MLA_HEADS // 2):
            pair = jnp.concatenate([o[(2 * p) * tq:(2 * p + 1) * tq], o[(2 * p + 1) * tq:(2 * p + 2) * tq]], axis=1)
            o_ref[0, :, p * LANES:(p + 1) * LANES] = _dot(pair.astype(_bf16), wuv_ref[p]).astype(o_ref.dtype)

    @pl.when((g < sel['total']) & (g % sel['nch'] == sel['nch'] - 1))
    def _():
        _select_finish(qt_ref, idx_ref, ksum_sc, dt=sel['dt'], nb=sel['nb'])


def _mla_prompt(qcat, kcat, w_uvp, bsz, seq, qt, kt_pool, page_table):
    tq, tk = min(MLA_TQ, seq), min(MLA_TK, seq)
    assert seq % tq == 0 and seq % tk == 0 and tk % tq == 0
    nq = seq // tq
    pairs = [(q, k) for q in range(nq) for k in range((q * tq) // tk + 1)]
    qi = jnp.asarray([p[0] for p in pairs], jnp.int32)
    ki = jnp.asarray([p[1] for p in pairs], jnp.int32)
    rows = MLA_HEADS * tq
    npairs = len(pairs)
    db, npg = page_table.shape
    dt = qt.shape[-1]
    cp, nch, nb = _select_plan(db, npg, bsz * npairs)
    sel = dict(total=db * nch, cp=cp, nch=nch, dt=dt, nb=nb)
    seq_of = lambda b, s: jnp.minimum((b * npairs + s) // nch, db - 1)
    grid_spec = pltpu.PrefetchScalarGridSpec(
        num_scalar_prefetch=3, grid=(bsz, npairs),
        in_specs=[pl.BlockSpec((MLA_HEADS, tq, MLA_QK), lambda b, s, qi, ki, pt: (0, b * nq + qi[s], 0)),
                  pl.BlockSpec((1, tk, MLA_QK), lambda b, s, qi, ki, pt: (b, ki[s], 0)),
                  pl.BlockSpec(w_uvp.shape, lambda b, s, qi, ki, pt: (0, 0, 0)),
                  pl.BlockSpec((1, MOBA_HEADS, MOBA_HEAD_DIM, dt), lambda b, s, qi, ki, pt: (seq_of(b, s), 0, 0, 0)),
                  pl.BlockSpec(memory_space=pl.ANY)],
        out_specs=[pl.BlockSpec((1, tq, MLA_HEADS * MLA_D_V), lambda b, s, qi, ki, pt: (b, qi[s], 0)),
                   pl.BlockSpec((1, MOBA_HEADS, dt, LANES), lambda b, s, qi, ki, pt: (seq_of(b, s), 0, 0, 0))],
        scratch_shapes=[pltpu.VMEM((rows, 1), _f32), pltpu.VMEM((rows, 1), _f32),
                        pltpu.VMEM((rows, MLA_KV_LORA), _f32),
                        pltpu.VMEM((2, cp, MOBA_HEADS, MOBA_HEAD_DIM, PAGE_SIZE), _f32),
                        pltpu.VMEM((MOBA_HEADS, MOBA_HEAD_DIM, LANES), _f32), pltpu.SemaphoreType.DMA((2,))])
    return pl.pallas_call(
        functools.partial(_mla_prompt_kernel, tq=tq, tk=tk, sel=sel), grid_spec=grid_spec,
        out_shape=[jax.ShapeDtypeStruct((bsz, seq, MLA_HEADS * MLA_D_V), _bf16),
                   jax.ShapeDtypeStruct((db, MOBA_HEADS, dt, LANES), jnp.int32)],
        compiler_params=pltpu.CompilerParams(dimension_semantics=("arbitrary", "arbitrary"),
                                             vmem_limit_bytes=VMEM_LIMIT),
    )(qi, ki, page_table, qcat, kcat.reshape(bsz, seq, MLA_QK), w_uvp, qt, kt_pool)


def _top_blocks(gate, lane, n_valid, first_lane):
    g = jnp.where((lane >= first_lane) & (lane < first_lane + n_valid), gate, -jnp.inf)
    lane_f = lane.astype(_f32)
    chosen = jnp.zeros(gate.shape, jnp.bool_)
    for _ in range(MOBA_TOPK):
        mx = jnp.max(g, axis=-1, keepdims=True)
        idx = jnp.min(jnp.where(g == mx, lane_f, 1e9), axis=-1, keepdims=True)
        pick = (lane_f == idx) & (mx > -jnp.inf)
        chosen = chosen | pick
        g = jnp.where(pick, -jnp.inf, g)
    return chosen


def _moba_prompt_kernel(qi_ref, kt_ref, q_ref, kaug_ref, v_ref, kmean_ref, o_ref, *scratch, tk):
    nhs = MOBA_HEADS
    qaug_sc, m_sc, acc_sc = scratch[:nhs], scratch[nhs:2 * nhs], scratch[2 * nhs:]
    s_id = pl.program_id(1)
    j, kt = qi_ref[s_id], kt_ref[s_id]
    t = MOBA_BLOCK
    nh = MOBA_HEADS

    @pl.when(kt == 0)
    def _():
        lane = lax.broadcasted_iota(jnp.int32, (t, LANES), 1)
        for h in range(nh):
            q = q_ref[h]
            gate = _dot_nt(q, kmean_ref[0, h])
            chosen = _top_blocks(gate, lane, j, MOBA_HEAD_DIM) | (lane == MOBA_HEAD_DIM + j)
            bias = jnp.where(chosen, 0.0, NEG_BIG)
            q_wide = jnp.concatenate([q.astype(_f32), jnp.zeros((t, MOBA_HEAD_DIM), _f32)], axis=1)
            qaug_sc[h][...] = jnp.where(lane < MOBA_HEAD_DIM, q_wide, bias).astype(_bf16)
            m_sc[h][...] = jnp.full(m_sc[h].shape, -jnp.inf, _f32)
            acc_sc[h][...] = jnp.zeros(acc_sc[h].shape, _f32)

    def update(causal):
        for h in range(nh):
            s = _dot_nt(qaug_sc[h][...], kaug_ref[h])
            if causal is not None:
                s = jnp.where(causal, s, -jnp.inf)
            m_prev = m_sc[h][...]
            m_new = jnp.maximum(m_prev, jnp.max(s, axis=-1, keepdims=True))
            alpha = jnp.exp2(m_prev - m_new)
            p = jnp.exp2(s - m_new)
            acc_sc[h][...] = alpha * acc_sc[h][...] + _dot(p.astype(_bf16), v_ref[h])
            m_sc[h][...] = m_new

    last = (j * t) // tk

    @pl.when(kt < last)
    def _():
        update(None)

    @pl.when(kt == last)
    def _():
        qpos = j * t + lax.broadcasted_iota(jnp.int32, (t, tk), 0)
        kpos = kt * tk + lax.broadcasted_iota(jnp.int32, (t, tk), 1)
        update(kpos <= qpos)
        outs = []
        for h in range(nh):
            acc = acc_sc[h][...]
            outs.append(acc[:, :MOBA_HEAD_DIM] / acc[:, MOBA_HEAD_DIM:MOBA_HEAD_DIM + 1])
        o_ref[0] = jnp.concatenate(outs, axis=1).astype(o_ref.dtype)


def _moba_prompt(qbh, kaug, vbh, kmean, bsz, seq):
    t = MOBA_BLOCK
    tk = min(MOBA_TK, seq)
    assert seq % tk == 0 and tk % t == 0
    nb = seq // t
    assert nb <= LANES - MOBA_HEAD_DIM
    nh = MOBA_HEADS
    km = kmean.reshape(bsz, nb, nh, MOBA_HEAD_DIM).transpose(0, 2, 1, 3)
    km = jnp.pad(km, ((0, 0), (0, 0), (MOBA_HEAD_DIM, LANES - MOBA_HEAD_DIM - nb), (0, 0))).astype(_bf16)
    pairs = [(q, k) for q in range(nb) for k in range((q * t) // tk + 1)]
    qi = jnp.asarray([p[0] for p in pairs], jnp.int32)
    kt = jnp.asarray([p[1] for p in pairs], jnp.int32)
    ntk = seq // tk
    qmap = lambda b, s, qi, kt: (0, b * nb + qi[s], 0)
    kmap = lambda b, s, qi, kt: (0, b * ntk + kt[s], 0)
    grid_spec = pltpu.PrefetchScalarGridSpec(
        num_scalar_prefetch=2, grid=(bsz, len(pairs)),
        in_specs=[pl.BlockSpec((nh, t, MOBA_HEAD_DIM), qmap),
                  pl.BlockSpec((nh, tk, LANES), kmap),
                  pl.BlockSpec((nh, tk, LANES), kmap),
                  pl.BlockSpec((1, nh, LANES, MOBA_HEAD_DIM), lambda b, s, qi, kt: (b, 0, 0, 0))],
        out_specs=pl.BlockSpec((1, t, MOBA_WIDTH), lambda b, s, qi, kt: (b, qi[s], 0)),
        scratch_shapes=([pltpu.VMEM((t, LANES), _bf16)] * nh + [pltpu.VMEM((t, 1), _f32)] * nh
                        + [pltpu.VMEM((t, LANES), _f32)] * nh))
    return pl.pallas_call(
        functools.partial(_moba_prompt_kernel, tk=tk), grid_spec=grid_spec,
        out_shape=jax.ShapeDtypeStruct((bsz, seq, MOBA_WIDTH), _bf16),
        compiler_params=pltpu.CompilerParams(dimension_semantics=("parallel", "arbitrary"),
                                             vmem_limit_bytes=VMEM_LIMIT),
    )(qi, kt, qbh, kaug, vbh, km)


def _prep_attn_weights(g_attn, w_in, g_q, w_uq, g_kv, w_uk, w_uv):
    c0, c1, c2 = MLA_Q_LORA, MLA_Q_LORA + MLA_KV_LORA, MLA_Q_LORA + MLA_KV_LORA + MLA_D_ROPE
    w_in_r = jnp.concatenate([w_in[:, :c1], w_in[:, c2:], w_in[:, c1:c2],
                              jnp.zeros((D_MODEL, LANES - MLA_D_ROPE), w_in.dtype)], axis=1)
    w_uqn = w_uq[:, :, :MLA_D_NOPE].reshape(MLA_Q_LORA, MLA_HEADS * MLA_D_NOPE)
    w_uqr = w_uq[:, :, MLA_D_NOPE:].reshape(MLA_Q_LORA, MLA_HEADS * MLA_D_ROPE)
    ukt = jnp.transpose(w_uk, (1, 2, 0))
    zk = jnp.zeros_like(ukt[0])
    w_ukp = jnp.stack([jnp.concatenate([jnp.concatenate([ukt[2 * p], zk], axis=1),
                                        jnp.concatenate([zk, ukt[2 * p + 1]], axis=1)], axis=0)
                       for p in range(MLA_HEADS // 2)])
    uv = jnp.transpose(w_uv, (1, 0, 2))
    zv = jnp.zeros_like(uv[0])
    w_uvp = jnp.stack([jnp.concatenate([jnp.concatenate([uv[2 * p], zv], axis=1),
                                        jnp.concatenate([zv, uv[2 * p + 1]], axis=1)], axis=0)
                       for p in range(MLA_HEADS // 2)])
    return {'g_attn': g_attn.reshape(1, -1), 'w_in': w_in_r.astype(_bf16), 'g_q': g_q.reshape(1, -1),
            'w_uqn': w_uqn.astype(_bf16), 'w_uqr': w_uqr.astype(_bf16), 'g_kv': g_kv.reshape(1, -1),
            'w_ukp': w_ukp.astype(_bf16), 'w_uvp': w_uvp.astype(_bf16)}


def _prep_post_weights(w_out, g_ffn, w_gr, b_gr, w_er, b_er, g_ple, w_ple_gate, w_ple_proj, g_final):
    pad = LANES - MOE_EXPERTS - MOE_GROUPS
    w_r = jnp.concatenate([w_er, w_gr, jnp.zeros((D_MODEL, pad), w_er.dtype)], axis=1)
    b_r = jnp.concatenate([b_er, b_gr, jnp.zeros((pad,), b_er.dtype)]).reshape(1, LANES)
    w_r_hi = w_r.astype(_bf16)
    w_r_lo = (w_r - w_r_hi.astype(_f32)).astype(_bf16)
    return {'w_out': w_out.astype(_bf16), 'g_ffn': g_ffn.reshape(1, -1), 'w_r_hi': w_r_hi, 'w_r_lo': w_r_lo,
            'b_r': b_r, 'g_ple': g_ple.reshape(1, -1), 'w_ple_gate': w_ple_gate.astype(_bf16),
            'w_ple_proj': w_ple_proj.astype(_bf16), 'g_final': g_final.reshape(1, -1)}


def _route_kernel(x_ref, omla_ref, omoba_ref, cnt0_ref, w_out_ref, g_ffn_ref, wr_hi_ref, wr_lo_ref, br_ref,
                  x1_ref, h2_ref, ri_ref, rw_ref, cnt_ref, cnt_sc):
    tm = x_ref.shape[0]

    @pl.when(pl.program_id(0) == 0)
    def _():
        cnt_sc[...] = cnt0_ref[...]

    om = jnp.concatenate([omla_ref[...], omoba_ref[...]], axis=1)
    x1 = x_ref[...] + _dot(om, w_out_ref[...])
    x1_ref[...] = x1
    h2 = _rms(x1, g_ffn_ref[...])
    h2_ref[...] = h2
    hi = h2.astype(_bf16)
    lo = (h2 - hi.astype(_f32)).astype(_bf16)
    logits = _dot(hi, wr_hi_ref[...]) + _dot(lo, wr_hi_ref[...]) + _dot(hi, wr_lo_ref[...]) + br_ref[...]

    lane = lax.broadcasted_iota(jnp.int32, (tm, LANES), 1)
    big = jnp.int32(2 ** 30)
    first = lambda mask: jnp.min(jnp.where(mask, lane, big), axis=-1, keepdims=True)
    gl = jnp.where((lane >= MOE_EXPERTS) & (lane < MOE_EXPERTS + MOE_GROUPS), logits, -jnp.inf)
    gmax = jnp.max(gl, axis=-1, keepdims=True)
    g_w = 1.0 / jnp.sum(jnp.exp(gl - gmax), axis=-1, keepdims=True)
    g_idx = first(gl == gmax) - MOE_EXPERTS
    e0 = g_idx * MOE_EXPERTS_PER_GROUP
    el = jnp.where((lane >= e0) & (lane < e0 + MOE_EXPERTS_PER_GROUP), logits, -jnp.inf)
    emax = jnp.max(el, axis=-1, keepdims=True)
    esum = jnp.sum(jnp.exp(el - emax), axis=-1, keepdims=True)
    i1 = first(el == emax)
    el2 = jnp.where(lane == i1, -jnp.inf, el)
    e2max = jnp.max(el2, axis=-1, keepdims=True)
    i2 = first(el2 == e2max)
    p1 = 1.0 / esum
    p2 = jnp.exp(e2max - emax) / esum
    w1 = g_w * p1 / (p1 + p2)
    w2 = g_w * p2 / (p1 + p2)
    oh1, oh2 = lane == i1, lane == i2
    both = jnp.where(oh1 | oh2, 1.0, 0.0)
    rowi = lax.broadcasted_iota(jnp.int32, (tm, tm), 0)
    coli = lax.broadcasted_iota(jnp.int32, (tm, tm), 1)
    tri = jnp.where(coli < rowi, 1.0, 0.0).astype(_bf16)
    before = cnt_sc[...] + _dot(tri, both.astype(_bf16))
    r1 = jnp.sum(jnp.where(oh1, before, 0.0), axis=-1, keepdims=True).astype(jnp.int32)
    r2 = jnp.sum(jnp.where(oh2, before, 0.0), axis=-1, keepdims=True).astype(jnp.int32)
    cnt = cnt_sc[...] + jnp.sum(both, axis=0, keepdims=True)
    cnt_sc[...] = cnt
    cnt_ref[...] = cnt
    l8 = lax.broadcasted_iota(jnp.int32, (tm, 8), 1)
    ri_ref[...] = jnp.where(l8 == 0, i1, jnp.where(l8 == 1, i2, jnp.where(l8 == 2, r1, jnp.where(l8 == 3, r2, 0))))
    rw_ref[...] = jnp.where(l8 == 0, w1, jnp.where(l8 == 1, w2, 0.0))


def _route(x, o_mla, o_moba, cnt0, w):
    n = x.shape[0]
    tm = ROW_TILE
    row = lambda wd: pl.BlockSpec((tm, wd), lambda i: (i, 0))
    full = lambda a: pl.BlockSpec(a.shape, lambda i: (0,) * a.ndim)
    ws = [cnt0, w['w_out'], w['g_ffn'], w['w_r_hi'], w['w_r_lo'], w['b_r']]
    return pl.pallas_call(
        _route_kernel, grid=(n // tm,),
        in_specs=[row(D_MODEL), row(MLA_HEADS * MLA_D_V), row(MOBA_WIDTH)] + [full(a) for a in ws],
        out_specs=[row(D_MODEL), row(D_MODEL), row(8), row(8), pl.BlockSpec((1, LANES), lambda i: (0, 0))],
        out_shape=[jax.ShapeDtypeStruct((n, D_MODEL), _f32), jax.ShapeDtypeStruct((n, D_MODEL), _f32),
                   jax.ShapeDtypeStruct((n, 8), jnp.int32), jax.ShapeDtypeStruct((n, 8), _f32),
                   jax.ShapeDtypeStruct((1, LANES), _f32)],
        scratch_shapes=[pltpu.VMEM((1, LANES), _f32)],
        compiler_params=pltpu.CompilerParams(dimension_semantics=("arbitrary",), vmem_limit_bytes=VMEM_LIMIT),
    )(x, o_mla, o_moba, *ws)


def _scatter_kernel(dest_ref, h_ref, xs_in_ref, xs_ref, sem):
    del xs_in_ref
    tm = h_ref.shape[0]

    def start(r, c):
        for k in range(2):
            pltpu.make_async_copy(h_ref.at[pl.ds(r, 1)], xs_ref.at[pl.ds(dest_ref[2 * r + k], 1)], sem).start()
        return c

    lax.fori_loop(0, tm, start, 0, unroll=8)
    for _ in range(2 * tm):
        pltpu.make_async_copy(h_ref.at[pl.ds(0, 1)], xs_ref.at[pl.ds(0, 1)], sem).wait()


def _scatter_rows(h2, dest, xs):
    n = h2.shape[0]
    tm = ROW_TILE
    return pl.pallas_call(
        _scatter_kernel, grid=(n // tm,),
        in_specs=[pl.BlockSpec((2 * tm,), lambda i: (i,), memory_space=pltpu.SMEM),
                  pl.BlockSpec((tm, D_MODEL), lambda i: (i, 0)),
                  pl.BlockSpec(memory_space=pl.ANY)],
        out_specs=pl.BlockSpec(memory_space=pl.ANY),
        out_shape=jax.ShapeDtypeStruct(xs.shape, xs.dtype),
        scratch_shapes=[pltpu.SemaphoreType.DMA(())],
        input_output_aliases={2: 0},
        compiler_params=pltpu.CompilerParams(dimension_semantics=("arbitrary",), has_side_effects=True),
    )(dest.reshape(-1), h2, xs)


def _expert_kernel(be_ref, na_ref, xs_ref, wg_ref, wu_ref, wd_ref, yb_ref, wg_sc, wu_sc, wd_sc):
    b = pl.program_id(0)

    @pl.when(b < na_ref[0])
    def _():
        prev = be_ref[jnp.maximum(b - 1, 0)]

        @pl.when((b == 0) | (be_ref[b] != prev))
        def _():
            wg_sc[...] = wg_ref[0].astype(_bf16)
            wu_sc[...] = wu_ref[0].astype(_bf16)
            wd_sc[...] = wd_ref[0].astype(_bf16)

        xi = xs_ref[...].astype(_bf16)
        g = _dot(xi, wg_sc[...])
        u = _dot(xi, wu_sc[...])
        a = g * (1.0 / (1.0 + jnp.exp(-g))) * u
        yb_ref[...] = _dot(a.astype(_bf16), wd_sc[...])

    @pl.when(b >= na_ref[0])
    def _():
        yb_ref[...] = jnp.zeros(yb_ref.shape, yb_ref.dtype)


def _experts(xs, blk_exp, n_active, w_eg, w_eu, w_ed):
    rows = xs.shape[0]
    r = MOE_ROWS
    nblk = rows // r
    blk = lambda b, be, na: (jnp.minimum(b, na[0] - 1), 0)
    wmap = lambda b, be, na: (be[jnp.minimum(b, na[0] - 1)], 0, 0)
    grid_spec = pltpu.PrefetchScalarGridSpec(
        num_scalar_prefetch=2, grid=(nblk,),
        in_specs=[pl.BlockSpec((r, D_MODEL), blk),
                  pl.BlockSpec((1, D_MODEL, D_EXPERT), wmap), pl.BlockSpec((1, D_MODEL, D_EXPERT), wmap),
                  pl.BlockSpec((1, D_EXPERT, D_MODEL), wmap)],
        out_specs=pl.BlockSpec((r, D_MODEL), lambda b, be, na: (b, 0)),
        scratch_shapes=[pltpu.VMEM((D_MODEL, D_EXPERT), _bf16), pltpu.VMEM((D_MODEL, D_EXPERT), _bf16),
                        pltpu.VMEM((D_EXPERT, D_MODEL), _bf16)])
    return pl.pallas_call(
        _expert_kernel, grid_spec=grid_spec, out_shape=jax.ShapeDtypeStruct((rows, D_MODEL), _f32),
        compiler_params=pltpu.CompilerParams(dimension_semantics=("arbitrary",), vmem_limit_bytes=VMEM_LIMIT),
    )(blk_exp, n_active, xs, w_eg, w_eu, w_ed)


def _combine_kernel(dest_ref, dnext_ref, yb_ref, x1_ref, rw_ref, p_ref, g_ple_ref, w_pg_ref, w_pp_ref, g_fin_ref,
                    y_ref, ybuf, sem, *, final_norm):
    tm = x1_ref.shape[0]
    i = pl.program_id(0)
    slot = i % 2

    def gather(d_ref, sl):
        def start(r, c):
            for k in range(2):
                pltpu.make_async_copy(yb_ref.at[pl.ds(d_ref[2 * r + k], 1)], ybuf.at[sl, k, pl.ds(r, 1)],
                                      sem.at[sl]).start()
            return c
        lax.fori_loop(0, tm, start, 0, unroll=8)

    @pl.when(i == 0)
    def _():
        gather(dest_ref, slot)

    @pl.when(i + 1 < pl.num_programs(0))
    def _():
        gather(dnext_ref, 1 - slot)

    ple = _dot(p_ref[...].astype(_bf16), w_pp_ref[...])
    for _ in range(2 * tm):
        pltpu.make_async_copy(yb_ref.at[pl.ds(0, 1)], ybuf.at[slot, 0, pl.ds(0, 1)], sem.at[slot]).wait()
    rw = rw_ref[...]
    x2 = x1_ref[...] + (ybuf[slot, 0] * rw[:, 0:1] + ybuf[slot, 1] * rw[:, 1:2])
    gate = _dot(_rms(x2, g_ple_ref[...]).astype(_bf16), w_pg_ref[...])
    x3 = x2 + ple * (1.0 / (1.0 + jnp.exp(-gate)))
    y_ref[...] = _rms(x3, g_fin_ref[...]) if final_norm else x3


def _combine(yb, dest, x1, rw, p, w, final_norm):
    n = x1.shape[0]
    tm = ROW_TILE
    nt = n // tm
    row = lambda wd: pl.BlockSpec((tm, wd), lambda i: (i, 0))
    full = lambda a: pl.BlockSpec(a.shape, lambda i: (0,) * a.ndim)
    ws = [w['g_ple'], w['w_ple_gate'], w['w_ple_proj'], w['g_final']]
    dflat = dest.reshape(-1)
    return pl.pallas_call(
        functools.partial(_combine_kernel, final_norm=final_norm), grid=(nt,),
        in_specs=[pl.BlockSpec((2 * tm,), lambda i: (i,), memory_space=pltpu.SMEM),
                  pl.BlockSpec((2 * tm,), lambda i: (jnp.minimum(i + 1, nt - 1),), memory_space=pltpu.SMEM),
                  pl.BlockSpec(memory_space=pl.ANY), row(D_MODEL), row(8), row(PLE_DIM)] + [full(a) for a in ws],
        out_specs=row(D_MODEL),
        out_shape=jax.ShapeDtypeStruct((n, D_MODEL), _f32),
        scratch_shapes=[pltpu.VMEM((2, 2, tm, D_MODEL), _f32), pltpu.SemaphoreType.DMA((2,))],
        compiler_params=pltpu.CompilerParams(dimension_semantics=("arbitrary",), vmem_limit_bytes=VMEM_LIMIT),
    )(dflat, dflat, yb, x1, rw, p, *ws)


def _post(groups, w, w_eg, w_eu, w_ed, final_norm=True):
    r = MOE_ROWS
    cnt = jnp.zeros((1, LANES), _f32)
    routed = []
    for x, o_mla, o_moba, _ in groups:
        x1, h2, ri, rw, cnt = _route(x, o_mla, o_moba, cnt, w)
        routed.append((x1, h2, ri, rw))
    counts = cnt[0, :MOE_EXPERTS].astype(jnp.int32)
    padded = (counts + r - 1) // r * r
    pad_end = jnp.cumsum(padded)
    pad_start = pad_end - padded
    n_all = sum(g[0].shape[0] for g in groups)
    nblk = -(-(2 * n_all) // r) + MOE_EXPERTS
    first_row = jnp.arange(nblk, dtype=jnp.int32)[:, None] * r
    blk_exp = jnp.minimum(jnp.sum(pad_end[None, :] <= first_row, axis=1), MOE_EXPERTS - 1).astype(jnp.int32)
    n_active = (pad_end[-1:] // r).astype(jnp.int32)
    xs = jnp.zeros((nblk * r, D_MODEL), _f32)
    dests = []
    for x1, h2, ri, rw in routed:
        dest = pad_start[ri[:, 0:2]] + ri[:, 2:4]
        xs = _scatter_rows(h2, dest, xs)
        dests.append(dest)
    yb = _experts(xs, blk_exp, n_active, w_eg, w_eu, w_ed)
    return [_combine(yb, dest, x1, rw, g[3], w, final_norm) for (x1, h2, ri, rw), dest, g in zip(routed, dests, groups)]


def _paged_step(n_steps_inner):
    step = pl.program_id(0) * n_steps_inner + pl.program_id(1)
    return step, step % 2


def _run_paged(copies, step, slot, n_inner, total=None):
    if total is None:
        total = pl.num_programs(0) * n_inner

    @pl.when(step == 0)
    def _():
        for cp in copies(0, 0, slot, False):
            cp.start()

    @pl.when(step + 1 < total)
    def _():
        nxt = step + 1
        for cp in copies(nxt // n_inner, nxt % n_inner, 1 - slot, False):
            cp.start()

    for cp in copies(0, 0, slot, True):
        cp.wait()


def _mla_sample_kernel(pt_ref, q_ref, knew_ref, ckv_hbm, kpet_hbm, o_ref, cbuf, pbuf, m_sc, l_sc, acc_sc, sem,
                       *, cp, nch, dt):
    c = pl.program_id(1)
    step, slot = _paged_step(nch)

    def copies(bb, cc, sl, fixed):
        out = []
        for j in range(cp):
            page = 0 if fixed else pt_ref[bb, cc * cp + j]
            out.append(pltpu.make_async_copy(ckv_hbm.at[page], cbuf.at[sl, j], sem.at[0, sl]))
            out.append(pltpu.make_async_copy(kpet_hbm.at[page], pbuf.at[sl, j], sem.at[1, sl]))
        return out

    _run_paged(copies, step, slot, nch)

    @pl.when(c == 0)
    def _():
        m_sc[...] = jnp.full(m_sc.shape, -jnp.inf, _f32)
        l_sc[...] = jnp.zeros(l_sc.shape, _f32)
        acc_sc[...] = jnp.zeros(acc_sc.shape, _f32)

    def update(s, v):
        m_prev = m_sc[...]
        m_new = jnp.maximum(m_prev, jnp.max(s, axis=-1, keepdims=True))
        alpha = jnp.exp(m_prev - m_new)
        p = jnp.exp(s - m_new)
        l_sc[...] = alpha * l_sc[...] + jnp.sum(p, axis=-1, keepdims=True)
        acc_sc[...] = alpha * acc_sc[...] + _dot(p.astype(_bf16), v)
        m_sc[...] = m_new

    q = q_ref[0]
    q_lat, q_pe = q[:, :MLA_KV_LORA], q[:, MLA_KV_LORA:]
    kc = cbuf[slot].reshape(cp * PAGE_SIZE, MLA_KV_LORA).astype(_bf16)
    s_pe = jnp.concatenate([_dot(q_pe, pbuf[slot, j].astype(_bf16)) for j in range(cp)], axis=1)
    update(_dot_nt(q_lat, kc) + s_pe, kc)

    @pl.when(c == nch - 1)
    def _():
        knew = knew_ref[0]
        s_new = _dot_nt(q, knew)
        t_row = lax.broadcasted_iota(jnp.int32, s_new.shape, 0) % dt
        col = lax.broadcasted_iota(jnp.int32, s_new.shape, 1)
        update(jnp.where(col <= t_row, s_new, -jnp.inf), knew[:, :MLA_KV_LORA])
        o_ref[0] = acc_sc[...] / l_sc[...]


def _mla_sample(q_rows, k_new, ckv_pool, kpet_pool, page_table, dt):
    db, npg = page_table.shape
    cp = min(MLA_PAGES, npg)
    assert npg % cp == 0
    nch = npg // cp
    rows = MLA_HEADS * dt
    grid_spec = pltpu.PrefetchScalarGridSpec(
        num_scalar_prefetch=1, grid=(db, nch),
        in_specs=[pl.BlockSpec((1, rows, MLA_QK), lambda b, c, pt: (b, 0, 0)),
                  pl.BlockSpec((1,) + k_new.shape[1:], lambda b, c, pt: (b, 0, 0)),
                  pl.BlockSpec(memory_space=pl.ANY), pl.BlockSpec(memory_space=pl.ANY)],
        out_specs=pl.BlockSpec((1, rows, MLA_KV_LORA), lambda b, c, pt: (b, 0, 0)),
        scratch_shapes=[pltpu.VMEM((2, cp, PAGE_SIZE, MLA_KV_LORA), _f32),
                        pltpu.VMEM((2, cp, MLA_D_ROPE, PAGE_SIZE), _f32),
                        pltpu.VMEM((rows, 1), _f32), pltpu.VMEM((rows, 1), _f32),
                        pltpu.VMEM((rows, MLA_KV_LORA), _f32), pltpu.SemaphoreType.DMA((2, 2))])
    return pl.pallas_call(
        functools.partial(_mla_sample_kernel, cp=cp, nch=nch, dt=dt), grid_spec=grid_spec,
        out_shape=jax.ShapeDtypeStruct((db, rows, MLA_KV_LORA), _f32),
        compiler_params=pltpu.CompilerParams(dimension_semantics=("arbitrary", "arbitrary"),
                                             vmem_limit_bytes=VMEM_LIMIT),
    )(page_table, q_rows, k_new, ckv_pool, kpet_pool)


def _uv_kernel(o_ref, w_ref, out_ref):
    for p in range(MLA_HEADS // 2):
        pair = o_ref[:, 2 * p * MLA_KV_LORA:(2 * p + 2) * MLA_KV_LORA].astype(_bf16)
        out_ref[:, p * LANES:(p + 1) * LANES] = _dot(pair, w_ref[p]).astype(out_ref.dtype)


def _uv_project(o_tok, w_uvp):
    n = o_tok.shape[0]
    tm = ROW_TILE
    return pl.pallas_call(
        _uv_kernel, grid=(n // tm,),
        in_specs=[pl.BlockSpec((tm, MLA_HEADS * MLA_KV_LORA), lambda i: (i, 0)),
                  pl.BlockSpec(w_uvp.shape, lambda i: (0, 0, 0))],
        out_specs=pl.BlockSpec((tm, MLA_HEADS * MLA_D_V), lambda i: (i, 0)),
        out_shape=jax.ShapeDtypeStruct((n, MLA_HEADS * MLA_D_V), _bf16),
    )(o_tok, w_uvp)


def _top_idx(gate, lane, n_valid):
    g = jnp.where(lane < n_valid, gate, -jnp.inf)
    out = []
    for _ in range(MOBA_TOPK):
        mx = jnp.max(g, axis=-1, keepdims=True)
        idx = jnp.min(jnp.where(g == mx, lane, jnp.int32(2 ** 30)), axis=-1, keepdims=True)
        out.append(idx)
        g = jnp.where(lane == idx, -jnp.inf, g)
    return out


def _select_plan(db, npg, steps_available):
    ppb = MOBA_BLOCK // PAGE_SIZE
    nb = npg // ppb
    assert npg % ppb == 0 and MOBA_TOPK <= nb <= LANES
    cp = min(MOBA_PAGES, npg)
    while npg % cp or cp % ppb or db * (npg // cp) > steps_available:
        cp += ppb
        assert cp <= npg, "not enough host-kernel steps to stream the MoBA key cache"
    return cp, npg // cp, nb


def _select_fetch(step, total, pt_ref, kt_hbm, kbuf, sem, *, cp, nch):
    def copies(bb, cc, sl, fixed):
        return [pltpu.make_async_copy(kt_hbm.at[0 if fixed else pt_ref[bb, cc * cp + j]], kbuf.at[sl, j], sem.at[sl])
                for j in range(cp)]

    _run_paged(copies, step, step % 2, nch, total)


def _select_accumulate(step, kbuf, ksum_sc, *, cp, nch):
    c = step % nch
    slot = step % 2
    ppb = MOBA_BLOCK // PAGE_SIZE
    lane = lax.broadcasted_iota(jnp.int32, (MOBA_HEAD_DIM, LANES), 1)
    for h in range(MOBA_HEADS):
        acc = jnp.where(c == 0, 0.0, ksum_sc[h])
        for jb in range(cp // ppb):
            blk = kbuf[slot, ppb * jb, h]
            for pg in range(1, ppb):
                blk = blk + kbuf[slot, ppb * jb + pg, h]
            red = jnp.sum(blk, axis=-1, keepdims=True)
            acc = jnp.where(lane == c * (cp // ppb) + jb, red, acc)
        ksum_sc[h] = acc


def _select_finish(qt_ref, idx_ref, ksum_sc, *, dt, nb):
    lane_t = lax.broadcasted_iota(jnp.int32, (dt, LANES), 1)
    for h in range(MOBA_HEADS):
        km = ksum_sc[h] * (1.0 / MOBA_BLOCK)
        qt = qt_ref[0, h]
        gate = jnp.concatenate([jnp.sum(km * qt[:, t:t + 1], axis=0, keepdims=True) for t in range(dt)], axis=0)
        i0, i1, i2 = _top_idx(gate, lane_t, nb)
        idx_ref[0, h] = jnp.where(lane_t == 0, i0, jnp.where(lane_t == 1, i1, i2))


def _moba_sample_kernel(pt_ref, sel_ref, qt_ref, knt_ref, vnt_ref, kt_hbm, vt_hbm, o_ref, kbuf, vbuf, sem,
                        *, hg, nhg, dt):
    g = pl.program_id(1)
    step, slot = _paged_step(nhg)
    ppb = MOBA_BLOCK // PAGE_SIZE
    per_q = MOBA_TOPK * ppb

    def copies(bb, gg, sl, fixed):
        out = []
        for hl in range(hg):
            h = 0 if fixed else gg * hg + hl
            for t in range(dt):
                for k in range(MOBA_TOPK):
                    blk = 0 if fixed else sel_ref[((bb * MOBA_HEADS + h) * dt + t) * MOBA_TOPK + k]
                    for pg in range(ppb):
                        page = 0 if fixed else pt_ref[bb, blk * ppb + pg]
                        i = (hl * dt + t) * per_q + k * ppb + pg
                        out.append(pltpu.make_async_copy(kt_hbm.at[page, h], kbuf.at[sl, i], sem.at[0, sl]))
                        out.append(pltpu.make_async_copy(vt_hbm.at[page, h], vbuf.at[sl, i], sem.at[1, sl]))
        return out

    _run_paged(copies, step, slot, nhg)

    lane_t = lax.broadcasted_iota(jnp.int32, (1, dt), 1)
    lane_o = lax.broadcasted_iota(jnp.int32, (MOBA_HEAD_DIM, dt), 1)
    for hl in range(hg):
        qt, knt, vnt = qt_ref[0, hl], knt_ref[0, hl], vnt_ref[0, hl]
        o_cols = jnp.zeros((MOBA_HEAD_DIM, dt), _f32)
        for t in range(dt):
            qcol = qt[:, t:t + 1]
            base = (hl * dt + t) * per_q
            s_sel = [jnp.sum(kbuf[slot, base + r] * qcol, axis=0, keepdims=True) for r in range(per_q)]
            s_own = jnp.where(lane_t <= t, jnp.sum(knt * qcol, axis=0, keepdims=True), -jnp.inf)
            m_row = s_sel[0]
            for r in range(1, per_q):
                m_row = jnp.maximum(m_row, s_sel[r])
            m = jnp.maximum(jnp.max(m_row, axis=-1, keepdims=True), jnp.max(s_own, axis=-1, keepdims=True))
            e_sel = [jnp.exp(s - m) for s in s_sel]
            e_own = jnp.exp(s_own - m)
            e_row = e_sel[0]
            acc = vbuf[slot, base] * e_sel[0]
            for r in range(1, per_q):
                e_row = e_row + e_sel[r]
                acc = acc + vbuf[slot, base + r] * e_sel[r]
            l = jnp.sum(e_row, axis=-1, keepdims=True) + jnp.sum(e_own, axis=-1, keepdims=True)
            o_col = (jnp.sum(acc, axis=-1, keepdims=True) + jnp.sum(vnt * e_own, axis=-1, keepdims=True)) / l
            o_cols = jnp.where(lane_o == t, o_col, o_cols)
        o_ref[0, hl] = o_cols


def _moba_sample(qt, knt, vnt, sel, kt_pool, vt_pool, page_table, dt):
    db, npg = page_table.shape
    hg = MOBA_HEAD_GROUP
    nhg = MOBA_HEADS // hg
    nslab = hg * dt * MOBA_TOPK * (MOBA_BLOCK // PAGE_SIZE)
    col = pl.BlockSpec((1, hg, MOBA_HEAD_DIM, dt), lambda b, g, pt, sel: (b, g, 0, 0))
    grid_spec = pltpu.PrefetchScalarGridSpec(
        num_scalar_prefetch=2, grid=(db, nhg),
        in_specs=[col, col, col, pl.BlockSpec(memory_space=pl.ANY), pl.BlockSpec(memory_space=pl.ANY)],
        out_specs=col,
        scratch_shapes=[pltpu.VMEM((2, nslab, MOBA_HEAD_DIM, PAGE_SIZE), _f32),
                        pltpu.VMEM((2, nslab, MOBA_HEAD_DIM, PAGE_SIZE), _f32), pltpu.SemaphoreType.DMA((2, 2))])
    return pl.pallas_call(
        functools.partial(_moba_sample_kernel, hg=hg, nhg=nhg, dt=dt), grid_spec=grid_spec,
        out_shape=jax.ShapeDtypeStruct((db, MOBA_HEADS, MOBA_HEAD_DIM, dt), _f32),
        compiler_params=pltpu.CompilerParams(dimension_semantics=("arbitrary", "arbitrary"),
                                             vmem_limit_bytes=VMEM_LIMIT),
    )(page_table, sel, qt, knt, vnt, kt_pool, vt_pool)


def _head_cols(a, db, dt):
    return a.reshape(db, dt, MOBA_HEADS, MOBA_HEAD_DIM).transpose(0, 2, 3, 1)


def _sample_attention(qcat_s, ckv_s, kpe_s, qt, kb_s, vb_s, sel, cache_ckv, kpet_pool, kt_pool, vt_pool, page_table,
                      w_uvp, db, dt):
    n_s = db * dt
    q_rows = qcat_s.reshape(MLA_HEADS, db, dt, MLA_QK).transpose(1, 0, 2, 3).reshape(db, MLA_HEADS * dt, MLA_QK)
    k_new = jnp.concatenate([ckv_s, kpe_s], axis=1).astype(_bf16).reshape(db, dt, MLA_QK)
    assert dt <= 16
    k_new = jnp.pad(k_new, ((0, 0), (0, 16 - dt), (0, 0)))
    o_lat = _mla_sample(q_rows, k_new, cache_ckv, kpet_pool, page_table, dt)
    o_tok = o_lat.reshape(db, MLA_HEADS, dt, MLA_KV_LORA).transpose(0, 2, 1, 3).reshape(n_s, -1)
    o_mla = _uv_project(o_tok, w_uvp)
    knt, vnt = _head_cols(kb_s, db, dt), _head_cols(vb_s, db, dt)
    ot = _moba_sample(qt, knt, vnt, sel[..., :MOBA_TOPK].reshape(-1), kt_pool, vt_pool, page_table, dt)
    o_moba = ot.transpose(0, 3, 1, 2).reshape(n_s, MOBA_WIDTH).astype(_bf16)
    return o_mla, o_moba


def kernel(x_prompt, x_sample, cache_ckv, cache_kpe, cache_k, cache_v, page_table, p_prompt, p_sample, g_attn, w_in,
           g_q, w_uq, g_kv, w_uk, w_uv, w_out, g_ffn, w_group_router, b_group_router, w_expert_router,
           b_expert_router, w_exp_gate, w_exp_up, w_exp_down, g_ple, w_ple_gate, w_ple_proj, g_final):
    bsz, seq, _ = x_prompt.shape
    db, dt, _ = x_sample.shape
    depth = g_attn.shape[0]
    n_p, n_s = bsz * seq, db * dt
    past = page_table.shape[1] * PAGE_SIZE
    assert ROW_TILE % dt == 0 and n_s % ROW_TILE == 0
    pos_p = jnp.arange(seq, dtype=jnp.int32)
    pos_s = past + jnp.arange(ROW_TILE, dtype=jnp.int32) % dt
    xp = x_prompt.reshape(n_p, D_MODEL)
    xs = x_sample.reshape(n_s, D_MODEL)
    outs_p, outs_s = [], []
    for i in range(depth):
        wa = _prep_attn_weights(g_attn[i], w_in[i], g_q[i], w_uq[i], g_kv[i], w_uk[i], w_uv[i])
        wp = _prep_post_weights(w_out[i], g_ffn[i], w_group_router[i], b_group_router[i], w_expert_router[i],
                                b_expert_router[i], g_ple[i], w_ple_gate[i], w_ple_proj[i], g_final)
        kpet_pool = jnp.transpose(cache_kpe[i], (0, 2, 1))
        kt_pool = jnp.transpose(cache_k[i], (0, 2, 3, 1))
        vt_pool = jnp.transpose(cache_v[i], (0, 2, 3, 1))
        ckv, kpe, kb, vb, qcat, kcat, qbh, kaug, vbh, kmean = _project(xp, pos_p, wa, sample=False)
        ckv_s, kpe_s, kb_s, vb_s, qcat_s, qb_s = _project(xs, pos_s, wa, sample=True)
        qt = _head_cols(qb_s, db, dt)
        o_mla, sel = _mla_prompt(qcat, kcat, wa['w_uvp'], bsz, seq, qt, kt_pool, page_table)
        o_mla = o_mla.reshape(n_p, MLA_HEADS * MLA_D_V)
        o_moba = _moba_prompt(qbh, kaug, vbh, kmean, bsz, seq).reshape(n_p, MOBA_WIDTH)
        outs_p.append((ckv, kpe, kb, vb))
        o_mla_s, o_moba_s = _sample_attention(qcat_s, ckv_s, kpe_s, qt, kb_s, vb_s, sel, cache_ckv[i], kpet_pool,
                                              kt_pool, vt_pool, page_table, wa['w_uvp'], db, dt)
        outs_s.append((ckv_s, kpe_s, kb_s, vb_s))
        xp, xs = _post([(xp, o_mla, o_moba, p_prompt[i].reshape(n_p, PLE_DIM)),
                        (xs, o_mla_s, o_moba_s, p_sample[i].reshape(n_s, PLE_DIM))],
                       wp, w_exp_gate[i], w_exp_up[i], w_exp_down[i], final_norm=(i == depth - 1))
    stack = lambda outs, j, shp: jnp.stack([o[j].reshape(shp) for o in outs])
    hs = (MOBA_HEADS, MOBA_HEAD_DIM)
    return (xp.reshape(bsz, seq, D_MODEL), xs.reshape(db, dt, D_MODEL),
            stack(outs_p, 0, (bsz, seq, MLA_KV_LORA)), stack(outs_p, 1, (bsz, seq, MLA_D_ROPE)),
            stack(outs_p, 2, (bsz, seq) + hs), stack(outs_p, 3, (bsz, seq) + hs),
            stack(outs_s, 0, (db, dt, MLA_KV_LORA)), stack(outs_s, 1, (db, dt, MLA_D_ROPE)),
            stack(outs_s, 2, (db, dt) + hs), stack(outs_s, 3, (db, dt) + hs))
```

```python
import functools

import jax
import jax.numpy as jnp
import numpy as np
from jax import lax
from jax.experimental import pallas as pl
from jax.experimental.pallas import tpu as pltpu

D_MODEL = 1024
PAGE_SIZE = 128
MLA_HEADS = 8
MLA_Q_LORA = 256
MLA_KV_LORA = 128
MLA_D_NOPE = 64
MLA_D_ROPE = 32
MLA_D_V = 64
MLA_SCALE = (MLA_D_NOPE + MLA_D_ROPE) ** -0.5
MLA_QK = MLA_KV_LORA + MLA_D_ROPE
MOBA_HEADS = 8
MOBA_HEAD_DIM = 64
MOBA_WIDTH = MOBA_HEADS * MOBA_HEAD_DIM
MOBA_BLOCK = 256
MOBA_TOPK = 3
MOBA_SCALE = MOBA_HEAD_DIM ** -0.5
MOE_GROUPS = 4
MOE_EXPERTS_PER_GROUP = 8
MOE_EXPERTS = MOE_GROUPS * MOE_EXPERTS_PER_GROUP
D_EXPERT = 512
PLE_DIM = 256
ROPE_THETA = 10000.0
NORM_EPS = 1e-6
LOG2E = 1.4426950408889634

LANES = 128
NEG_BIG = -1e9

ROW_TILE = 256
MLA_TQ = 128
MLA_TK = 512
MOBA_TK = 1024
MOE_ROWS = 256
MLA_PAGES = 64
MOBA_PAGES = 16
MOBA_HEAD_GROUP = 4
VMEM_LIMIT = 48 * 1024 * 1024

_bf16 = jnp.bfloat16
_f32 = jnp.float32


def _dot(a, b):
    return jnp.dot(a, b, preferred_element_type=_f32)


def _dot_nt(a, b):
    return lax.dot_general(a, b, (((1,), (1,)), ((), ())), preferred_element_type=_f32)


def _rms(x, g):
    return x * lax.rsqrt(jnp.mean(x * x, axis=-1, keepdims=True) + NORM_EPS) * g


def _rope_apply(x, cos, sin_lo, sin_hi, half):
    w = x.shape[-1]
    return x * cos + pltpu.roll(x, w - half, 1) * sin_lo + pltpu.roll(x, half, 1) * sin_hi


def _proj_kernel(x_ref, g_attn_ref, w_in_ref, g_q_ref, w_uqn_ref, w_uqr_ref, g_kv_ref, w_ukp_ref,
                 cb_ref, slb_ref, shb_ref, ca_ref, sla_ref, sha_ref,
                 ckv_ref, kpe_ref, kb_ref, vb_ref, qcat_ref, *rest, sample, tiles_per_seq):
    tm = x_ref.shape[0]
    h = _rms(x_ref[...], g_attn_ref[...])
    z = _dot(h.astype(_bf16), w_in_ref[...])
    c_q = z[:, 0:256]
    c_kv = z[:, 256:384]
    q_b = z[:, 384:896]
    k_b = z[:, 896:1408]
    v_b = z[:, 1408:1920]
    k_pe = z[:, 1920:2048]

    cqn = _rms(c_q, g_q_ref[...]).astype(_bf16)
    q_nope = _dot(cqn, w_uqn_ref[...])
    q_rope = _dot(cqn, w_uqr_ref[...])
    wide = lambda ref, reps: jnp.concatenate([ref[...]] * reps, axis=1)
    reps_a = MLA_HEADS * MLA_D_ROPE // LANES
    ca, sla, sha = wide(ca_ref, reps_a), wide(sla_ref, reps_a), wide(sha_ref, reps_a)
    base2 = 1.0 if sample else LOG2E
    q_pe = _rope_apply(q_rope, ca, sla, sha, MLA_D_ROPE // 2) * (MLA_SCALE * base2)
    ckv = _rms(c_kv, g_kv_ref[...])
    kpe = _rope_apply(k_pe, ca[:, :LANES], sla[:, :LANES], sha[:, :LANES], MLA_D_ROPE // 2)[:, :MLA_D_ROPE]
    ckv_ref[...] = ckv
    kpe_ref[...] = kpe

    reps_b = MOBA_WIDTH // LANES
    cb, slb, shb = wide(cb_ref, reps_b), wide(slb_ref, reps_b), wide(shb_ref, reps_b)
    q_b = _rope_apply(q_b, cb, slb, shb, MOBA_HEAD_DIM // 2) * (MOBA_SCALE * base2)
    k_b = _rope_apply(k_b, cb, slb, shb, MOBA_HEAD_DIM // 2)
    kb_ref[...] = k_b
    vb_ref[...] = v_b

    for p in range(MLA_HEADS // 2):
        qn = q_nope[:, p * LANES:(p + 1) * LANES].astype(_bf16)
        ql = _dot(qn, w_ukp_ref[p]) * (MLA_SCALE * base2)
        for u in range(2):
            hd = 2 * p + u
            qcat_ref[hd, :, 0:MLA_KV_LORA] = ql[:, u * LANES:(u + 1) * LANES].astype(_bf16)
            qcat_ref[hd, :, MLA_KV_LORA:MLA_QK] = q_pe[:, hd * MLA_D_ROPE:(hd + 1) * MLA_D_ROPE].astype(_bf16)

    if sample:
        (qb_ref,) = rest
        qb_ref[...] = q_b
        return

    kcat_ref, qbh_ref, kaug_ref, vbh_ref, kmean_ref = rest
    kcat_ref[:, 0:MLA_KV_LORA] = ckv.astype(_bf16)
    kcat_ref[:, MLA_KV_LORA:MLA_QK] = kpe.astype(_bf16)
    kmean_ref[0] = jnp.sum(k_b, axis=0, keepdims=True) * (1.0 / MOBA_BLOCK)
    blk = pl.program_id(0) % tiles_per_seq
    lane = lax.broadcasted_iota(jnp.int32, (tm, LANES), 1)
    onehot = jnp.where(lane == MOBA_HEAD_DIM + blk, 1.0, 0.0)
    ones_col = jnp.where(lane == MOBA_HEAD_DIM, 1.0, 0.0)
    for p in range(MOBA_HEADS // 2):
        sl = slice(p * LANES, (p + 1) * LANES)
        qp, kp, vp = q_b[:, sl], k_b[:, sl], v_b[:, sl]
        kp_sw = pltpu.roll(kp, MOBA_HEAD_DIM, 1)
        vp_sw = pltpu.roll(vp, MOBA_HEAD_DIM, 1)
        for u, ksrc, vsrc in ((0, kp, vp), (1, kp_sw, vp_sw)):
            hd = 2 * p + u
            kaug_ref[hd] = jnp.where(lane < MOBA_HEAD_DIM, ksrc, onehot).astype(_bf16)
            vbh_ref[hd] = jnp.where(lane < MOBA_HEAD_DIM, vsrc, ones_col).astype(_bf16)
            qbh_ref[hd] = qp[:, u * MOBA_HEAD_DIM:(u + 1) * MOBA_HEAD_DIM].astype(_bf16)


def _rope_tables(pos, d, reps):
    half = d // 2
    inv = jnp.power(ROPE_THETA, -jnp.arange(half, dtype=_f32) * (2.0 / d))
    ang = pos.astype(_f32)[:, None] * inv[None, :]
    cos, sin, zero = jnp.cos(ang), jnp.sin(ang), jnp.zeros_like(ang)
    tile = lambda a, b: jnp.tile(jnp.concatenate([a, b], axis=1), (1, reps))
    return tile(cos, cos), tile(-sin, zero), tile(zero, sin)


def _project(x, pos, wts, *, sample):
    n = x.shape[0]
    tm = ROW_TILE
    assert n % tm == 0 and pos.shape[0] % tm == 0
    tiles_per_seq = pos.shape[0] // tm
    tabs_b = _rope_tables(pos, MOBA_HEAD_DIM, LANES // MOBA_HEAD_DIM)
    tabs_a = _rope_tables(pos, MLA_D_ROPE, LANES // MLA_D_ROPE)
    row = lambda w: pl.BlockSpec((tm, w), lambda i: (i, 0))
    full = lambda a: pl.BlockSpec(a.shape, lambda i: (0,) * a.ndim)
    tab = lambda w: pl.BlockSpec((tm, w), lambda i: (i % tiles_per_seq, 0))
    heads = lambda w: pl.BlockSpec((MLA_HEADS, tm, w), lambda i: (0, i, 0))
    out_shape = [jax.ShapeDtypeStruct((n, MLA_KV_LORA), _f32), jax.ShapeDtypeStruct((n, MLA_D_ROPE), _f32),
                 jax.ShapeDtypeStruct((n, MOBA_WIDTH), _f32), jax.ShapeDtypeStruct((n, MOBA_WIDTH), _f32),
                 jax.ShapeDtypeStruct((MLA_HEADS, n, MLA_QK), _bf16)]
    out_specs = [row(MLA_KV_LORA), row(MLA_D_ROPE), row(MOBA_WIDTH), row(MOBA_WIDTH), heads(MLA_QK)]
    if sample:
        out_shape += [jax.ShapeDtypeStruct((n, MOBA_WIDTH), _f32)]
        out_specs += [row(MOBA_WIDTH)]
    else:
        out_shape += [jax.ShapeDtypeStruct((n, MLA_QK), _bf16),
                      jax.ShapeDtypeStruct((MOBA_HEADS, n, MOBA_HEAD_DIM), _bf16),
                      jax.ShapeDtypeStruct((MOBA_HEADS, n, LANES), _bf16),
                      jax.ShapeDtypeStruct((MOBA_HEADS, n, LANES), _bf16),
                      jax.ShapeDtypeStruct((n // tm, 1, MOBA_WIDTH), _f32)]
        out_specs += [row(MLA_QK), heads(MOBA_HEAD_DIM), heads(LANES), heads(LANES),
                      pl.BlockSpec((1, 1, MOBA_WIDTH), lambda i: (i, 0, 0))]
    w = wts
    ins = [x, w['g_attn'], w['w_in'], w['g_q'], w['w_uqn'], w['w_uqr'], w['g_kv'], w['w_ukp'], *tabs_b, *tabs_a]
    in_specs = [row(D_MODEL)] + [full(a) for a in ins[1:8]] + [tab(LANES)] * 6
    return pl.pallas_call(
        functools.partial(_proj_kernel, sample=sample, tiles_per_seq=tiles_per_seq),
        grid=(n // tm,), in_specs=in_specs, out_specs=out_specs, out_shape=out_shape,
        compiler_params=pltpu.CompilerParams(dimension_semantics=("parallel",), vmem_limit_bytes=VMEM_LIMIT),
    )(*ins)


def _mla_prompt_kernel(qi_ref, ki_ref, pt_ref, q_ref, k_ref, wuv_ref, qt_ref, kt_hbm, o_ref, idx_ref,
                       m_sc, l_sc, acc_sc, kbuf, ksum_sc, sem, *, tq, tk, sel):
    s_id = pl.program_id(1)
    qi, ki = qi_ref[s_id], ki_ref[s_id]
    rows = MLA_HEADS * tq
    last_k = (qi * tq) // tk

    g = pl.program_id(0) * pl.num_programs(1) + s_id

    @pl.when(g < sel['total'])
    def _():
        _select_fetch(g, sel['total'], pt_ref, kt_hbm, kbuf, sem, cp=sel['cp'], nch=sel['nch'])

    @pl.when(ki == 0)
    def _():
        m_sc[...] = jnp.full(m_sc.shape, -jnp.inf, _f32)
        l_sc[...] = jnp.zeros(l_sc.shape, _f32)
        acc_sc[...] = jnp.zeros(acc_sc.shape, _f32)

    def update(causal):
        _select_accumulate(g, kbuf, ksum_sc, cp=sel['cp'], nch=sel['nch'])
        q = q_ref[...].reshape(rows, MLA_QK)
        k = k_ref[0]
        s = _dot_nt(q, k)
        if causal is not None:
            s = jnp.where(causal, s, -jnp.inf)
        m_prev = m_sc[...]
        m_new = jnp.maximum(m_prev, jnp.max(s, axis=-1, keepdims=True))
        alpha = jnp.exp2(m_prev - m_new)
        p = jnp.exp2(s - m_new)
        l_sc[...] = alpha * l_sc[...] + jnp.sum(p, axis=-1, keepdims=True)
        acc_sc[...] = alpha * acc_sc[...] + _dot(p.astype(_bf16), k[:, :MLA_KV_LORA])
        m_sc[...] = m_new

    @pl.when(ki < last_k)
    def _():
        update(None)

    @pl.when(ki == last_k)
    def _():
        qpos = qi * tq + lax.broadcasted_iota(jnp.int32, (rows, tk), 0) % tq
        kpos = ki * tk + lax.broadcasted_iota(jnp.int32, (rows, tk), 1)
        update(kpos <= qpos)
        o = acc_sc[...] / l_sc[...]
        for p in range(MLA_HEADS // 2):
            pair = jnp.concatenate([o[(2 * p) * tq:(2 * p + 1) * tq], o[(2 * p + 1) * tq:(2 * p + 2) * tq]], axis=1)
            o_ref[0, :, p * LANES:(p + 1) * LANES] = _dot(pair.astype(_bf16), wuv_ref[p]).astype(o_ref.dtype)

    @pl.when((g < sel['total']) & (g % sel['nch'] == sel['nch'] - 1))
    def _():
        _select_finish(qt_ref, idx_ref, ksum_sc, dt=sel['dt'], nb=sel['nb'])


def _mla_prompt(qcat, kcat, w_uvp, bsz, seq, qt, kt_pool, page_table):
    tq, tk = min(MLA_TQ, seq), min(MLA_TK, seq)
    assert seq % tq == 0 and seq % tk == 0 and tk % tq == 0
    nq = seq // tq
    pairs = [(q, k) for q in range(nq) for k in range((q * tq) // tk + 1)]
    qi = jnp.asarray([p[0] for p in pairs], jnp.int32)
    ki = jnp.asarray([p[1] for p in pairs], jnp.int32)
    rows = MLA_HEADS * tq
    npairs = len(pairs)
    db, npg = page_table.shape
    dt = qt.shape[-1]
    cp, nch, nb = _select_plan(db, npg, bsz * npairs)
    sel = dict(total=db * nch, cp=cp, nch=nch, dt=dt, nb=nb)
    seq_of = lambda b, s: jnp.minimum((b * npairs + s) // nch, db - 1)
    grid_spec = pltpu.PrefetchScalarGridSpec(
        num_scalar_prefetch=3, grid=(bsz, npairs),
        in_specs=[pl.BlockSpec((MLA_HEADS, tq, MLA_QK), lambda b, s, qi, ki, pt: (0, b * nq + qi[s], 0)),
                  pl.BlockSpec((1, tk, MLA_QK), lambda b, s, qi, ki, pt: (b, ki[s], 0)),
                  pl.BlockSpec(w_uvp.shape, lambda b, s, qi, ki, pt: (0, 0, 0)),
                  pl.BlockSpec((1, MOBA_HEADS, MOBA_HEAD_DIM, dt), lambda b, s, qi, ki, pt: (seq_of(b, s), 0, 0, 0)),
                  pl.BlockSpec(memory_space=pl.ANY)],
        out_specs=[pl.BlockSpec((1, tq, MLA_HEADS * MLA_D_V), lambda b, s, qi, ki, pt: (b, qi[s], 0)),
                   pl.BlockSpec((1, MOBA_HEADS, dt, LANES), lambda b, s, qi, ki, pt: (seq_of(b, s), 0, 0, 0))],
        scratch_shapes=[pltpu.VMEM((rows, 1), _f32), pltpu.VMEM((rows, 1), _f32),
                        pltpu.VMEM((rows, MLA_KV_LORA), _f32),
                        pltpu.VMEM((2, cp, MOBA_HEADS, MOBA_HEAD_DIM, PAGE_SIZE), _f32),
                        pltpu.VMEM((MOBA_HEADS, MOBA_HEAD_DIM, LANES), _f32), pltpu.SemaphoreType.DMA((2,))])
    return pl.pallas_call(
        functools.partial(_mla_prompt_kernel, tq=tq, tk=tk, sel=sel), grid_spec=grid_spec,
        out_shape=[jax.ShapeDtypeStruct((bsz, seq, MLA_HEADS * MLA_D_V), _bf16),
                   jax.ShapeDtypeStruct((db, MOBA_HEADS, dt, LANES), jnp.int32)],
        compiler_params=pltpu.CompilerParams(dimension_semantics=("arbitrary", "arbitrary"),
                                             vmem_limit_bytes=VMEM_LIMIT),
    )(qi, ki, page_table, qcat, kcat.reshape(bsz, seq, MLA_QK), w_uvp, qt, kt_pool)


def _top_blocks(gate, lane, n_valid, first_lane):
    g = jnp.where((lane >= first_lane) & (lane < first_lane + n_valid), gate, -jnp.inf)
    lane_f = lane.astype(_f32)
    chosen = jnp.zeros(gate.shape, jnp.bool_)
    for _ in range(MOBA_TOPK):
        mx = jnp.max(g, axis=-1, keepdims=True)
        idx = jnp.min(jnp.where(g == mx, lane_f, 1e9), axis=-1, keepdims=True)
        pick = (lane_f == idx) & (mx > -jnp.inf)
        chosen = chosen | pick
        g = jnp.where(pick, -jnp.inf, g)
    return chosen


def _moba_prompt_kernel(qi_ref, kt_ref, q_ref, kaug_ref, v_ref, kmean_ref, o_ref, *scratch, tk):
    nhs = MOBA_HEADS
    qaug_sc, m_sc, acc_sc = scratch[:nhs], scratch[nhs:2 * nhs], scratch[2 * nhs:]
    s_id = pl.program_id(1)
    j, kt = qi_ref[s_id], kt_ref[s_id]
    t = MOBA_BLOCK
    nh = MOBA_HEADS

    @pl.when(kt == 0)
    def _():
        lane = lax.broadcasted_iota(jnp.int32, (t, LANES), 1)
        for h in range(nh):
            q = q_ref[h]
            gate = _dot_nt(q, kmean_ref[0, h])
            chosen = _top_blocks(gate, lane, j, MOBA_HEAD_DIM) | (lane == MOBA_HEAD_DIM + j)
            bias = jnp.where(chosen, 0.0, NEG_BIG)
            q_wide = jnp.concatenate([q.astype(_f32), jnp.zeros((t, MOBA_HEAD_DIM), _f32)], axis=1)
            qaug_sc[h][...] = jnp.where(lane < MOBA_HEAD_DIM, q_wide, bias).astype(_bf16)
            m_sc[h][...] = jnp.full(m_sc[h].shape, -jnp.inf, _f32)
            acc_sc[h][...] = jnp.zeros(acc_sc[h].shape, _f32)

    def update(causal):
        for h in range(nh):
            s = _dot_nt(qaug_sc[h][...], kaug_ref[h])
            if causal is not None:
                s = jnp.where(causal, s, -jnp.inf)
            m_prev = m_sc[h][...]
            m_new = jnp.maximum(m_prev, jnp.max(s, axis=-1, keepdims=True))
            alpha = jnp.exp2(m_prev - m_new)
            p = jnp.exp2(s - m_new)
            acc_sc[h][...] = alpha * acc_sc[h][...] + _dot(p.astype(_bf16), v_ref[h])
            m_sc[h][...] = m_new

    last = (j * t) // tk

    @pl.when(kt < last)
    def _():
        update(None)

    @pl.when(kt == last)
    def _():
        qpos = j * t + lax.broadcasted_iota(jnp.int32, (t, tk), 0)
        kpos = kt * tk + lax.broadcasted_iota(jnp.int32, (t, tk), 1)
        update(kpos <= qpos)
        outs = []
        for h in range(nh):
            acc = acc_sc[h][...]
            outs.append(acc[:, :MOBA_HEAD_DIM] / acc[:, MOBA_HEAD_DIM:MOBA_HEAD_DIM + 1])
        o_ref[0] = jnp.concatenate(outs, axis=1).astype(o_ref.dtype)


def _moba_prompt(qbh, kaug, vbh, kmean, bsz, seq):
    t = MOBA_BLOCK
    tk = min(MOBA_TK, seq)
    assert seq % tk == 0 and tk % t == 0
    nb = seq // t
    assert nb <= LANES - MOBA_HEAD_DIM
    nh = MOBA_HEADS
    km = kmean.reshape(bsz, nb, nh, MOBA_HEAD_DIM).transpose(0, 2, 1, 3)
    km = jnp.pad(km, ((0, 0), (0, 0), (MOBA_HEAD_DIM, LANES - MOBA_HEAD_DIM - nb), (0, 0))).astype(_bf16)
    pairs = [(q, k) for q in range(nb) for k in range((q * t) // tk + 1)]
    qi = jnp.asarray([p[0] for p in pairs], jnp.int32)
    kt = jnp.asarray([p[1] for p in pairs], jnp.int32)
    ntk = seq // tk
    qmap = lambda b, s, qi, kt: (0, b * nb + qi[s], 0)
    kmap = lambda b, s, qi, kt: (0, b * ntk + kt[s], 0)
    grid_spec = pltpu.PrefetchScalarGridSpec(
        num_scalar_prefetch=2, grid=(bsz, len(pairs)),
        in_specs=[pl.BlockSpec((nh, t, MOBA_HEAD_DIM), qmap),
                  pl.BlockSpec((nh, tk, LANES), kmap),
                  pl.BlockSpec((nh, tk, LANES), kmap),
                  pl.BlockSpec((1, nh, LANES, MOBA_HEAD_DIM), lambda b, s, qi, kt: (b, 0, 0, 0))],
        out_specs=pl.BlockSpec((1, t, MOBA_WIDTH), lambda b, s, qi, kt: (b, qi[s], 0)),
        scratch_shapes=([pltpu.VMEM((t, LANES), _bf16)] * nh + [pltpu.VMEM((t, 1), _f32)] * nh
                        + [pltpu.VMEM((t, LANES), _f32)] * nh))
    return pl.pallas_call(
        functools.partial(_moba_prompt_kernel, tk=tk), grid_spec=grid_spec,
        out_shape=jax.ShapeDtypeStruct((bsz, seq, MOBA_WIDTH), _bf16),
        compiler_params=pltpu.CompilerParams(dimension_semantics=("parallel", "arbitrary"),
                                             vmem_limit_bytes=VMEM_LIMIT),
    )(qi, kt, qbh, kaug, vbh, km)


def _prep_attn_weights(g_attn, w_in, g_q, w_uq, g_kv, w_uk, w_uv):
    c0, c1, c2 = MLA_Q_LORA, MLA_Q_LORA + MLA_KV_LORA, MLA_Q_LORA + MLA_KV_LORA + MLA_D_ROPE
    w_in_r = jnp.concatenate([w_in[:, :c1], w_in[:, c2:], w_in[:, c1:c2],
                              jnp.zeros((D_MODEL, LANES - MLA_D_ROPE), w_in.dtype)], axis=1)
    w_uqn = w_uq[:, :, :MLA_D_NOPE].reshape(MLA_Q_LORA, MLA_HEADS * MLA_D_NOPE)
    w_uqr = w_uq[:, :, MLA_D_NOPE:].reshape(MLA_Q_LORA, MLA_HEADS * MLA_D_ROPE)
    ukt = jnp.transpose(w_uk, (1, 2, 0))
    zk = jnp.zeros_like(ukt[0])
    w_ukp = jnp.stack([jnp.concatenate([jnp.concatenate([ukt[2 * p], zk], axis=1),
                                        jnp.concatenate([zk, ukt[2 * p + 1]], axis=1)], axis=0)
                       for p in range(MLA_HEADS // 2)])
    uv = jnp.transpose(w_uv, (1, 0, 2))
    zv = jnp.zeros_like(uv[0])
    w_uvp = jnp.stack([jnp.concatenate([jnp.concatenate([uv[2 * p], zv], axis=1),
                                        jnp.concatenate([zv, uv[2 * p + 1]], axis=1)], axis=0)
                       for p in range(MLA_HEADS // 2)])
    return {'g_attn': g_attn.reshape(1, -1), 'w_in': w_in_r.astype(_bf16), 'g_q': g_q.reshape(1, -1),
            'w_uqn': w_uqn.astype(_bf16), 'w_uqr': w_uqr.astype(_bf16), 'g_kv': g_kv.reshape(1, -1),
            'w_ukp': w_ukp.astype(_bf16), 'w_uvp': w_uvp.astype(_bf16)}


def _prep_post_weights(w_out, g_ffn, w_gr, b_gr, w_er, b_er, g_ple, w_ple_gate, w_ple_proj, g_final):
    pad = LANES - MOE_EXPERTS - MOE_GROUPS
    w_r = jnp.concatenate([w_er, w_gr, jnp.zeros((D_MODEL, pad), w_er.dtype)], axis=1)
    b_r = jnp.concatenate([b_er, b_gr, jnp.zeros((pad,), b_er.dtype)]).reshape(1, LANES)
    w_r_hi = w_r.astype(_bf16)
    w_r_lo = (w_r - w_r_hi.astype(_f32)).astype(_bf16)
    return {'w_out': w_out.astype(_bf16), 'g_ffn': g_ffn.reshape(1, -1), 'w_r_hi': w_r_hi, 'w_r_lo': w_r_lo,
            'b_r': b_r, 'g_ple': g_ple.reshape(1, -1), 'w_ple_gate': w_ple_gate.astype(_bf16),
            'w_ple_proj': w_ple_proj.astype(_bf16), 'g_final': g_final.reshape(1, -1)}


def _route_kernel(x_ref, omla_ref, omoba_ref, cnt0_ref, w_out_ref, g_ffn_ref, wr_hi_ref, wr_lo_ref, br_ref,
                  x1_ref, h2_ref, ri_ref, rw_ref, cnt_ref, cnt_sc):
    tm = x_ref.shape[0]

    @pl.when(pl.program_id(0) == 0)
    def _():
        cnt_sc[...] = cnt0_ref[...]

    om = jnp.concatenate([omla_ref[...], omoba_ref[...]], axis=1)
    x1 = x_ref[...] + _dot(om, w_out_ref[...])
    x1_ref[...] = x1
    h2 = _rms(x1, g_ffn_ref[...])
    h2_ref[...] = h2
    hi = h2.astype(_bf16)
    lo = (h2 - hi.astype(_f32)).astype(_bf16)
    logits = _dot(hi, wr_hi_ref[...]) + _dot(lo, wr_hi_ref[...]) + _dot(hi, wr_lo_ref[...]) + br_ref[...]

    lane = lax.broadcasted_iota(jnp.int32, (tm, LANES), 1)
    big = jnp.int32(2 ** 30)
    first = lambda mask: jnp.min(jnp.where(mask, lane, big), axis=-1, keepdims=True)
    gl = jnp.where((lane >= MOE_EXPERTS) & (lane < MOE_EXPERTS + MOE_GROUPS), logits, -jnp.inf)
    gmax = jnp.max(gl, axis=-1, keepdims=True)
    g_w = 1.0 / jnp.sum(jnp.exp(gl - gmax), axis=-1, keepdims=True)
    g_idx = first(gl == gmax) - MOE_EXPERTS
    e0 = g_idx * MOE_EXPERTS_PER_GROUP
    el = jnp.where((lane >= e0) & (lane < e0 + MOE_EXPERTS_PER_GROUP), logits, -jnp.inf)
    emax = jnp.max(el, axis=-1, keepdims=True)
    esum = jnp.sum(jnp.exp(el - emax), axis=-1, keepdims=True)
    i1 = first(el == emax)
    el2 = jnp.where(lane == i1, -jnp.inf, el)
    e2max = jnp.max(el2, axis=-1, keepdims=True)
    i2 = first(el2 == e2max)
    p1 = 1.0 / esum
    p2 = jnp.exp(e2max - emax) / esum
    w1 = g_w * p1 / (p1 + p2)
    w2 = g_w * p2 / (p1 + p2)
    oh1, oh2 = lane == i1, lane == i2
    both = jnp.where(oh1 | oh2, 1.0, 0.0)
    rowi = lax.broadcasted_iota(jnp.int32, (tm, tm), 0)
    coli = lax.broadcasted_iota(jnp.int32, (tm, tm), 1)
    tri = jnp.where(coli < rowi, 1.0, 0.0).astype(_bf16)
    before = cnt_sc[...] + _dot(tri, both.astype(_bf16))
    r1 = jnp.sum(jnp.where(oh1, before, 0.0), axis=-1, keepdims=True).astype(jnp.int32)
    r2 = jnp.sum(jnp.where(oh2, before, 0.0), axis=-1, keepdims=True).astype(jnp.int32)
    cnt = cnt_sc[...] + jnp.sum(both, axis=0, keepdims=True)
    cnt_sc[...] = cnt
    cnt_ref[...] = cnt
    l8 = lax.broadcasted_iota(jnp.int32, (tm, 8), 1)
    ri_ref[...] = jnp.where(l8 == 0, i1, jnp.where(l8 == 1, i2, jnp.where(l8 == 2, r1, jnp.where(l8 == 3, r2, 0))))
    rw_ref[...] = jnp.where(l8 == 0, w1, jnp.where(l8 == 1, w2, 0.0))


def _route(x, o_mla, o_moba, cnt0, w):
    n = x.shape[0]
    tm = ROW_TILE
    row = lambda wd: pl.BlockSpec((tm, wd), lambda i: (i, 0))
    full = lambda a: pl.BlockSpec(a.shape, lambda i: (0,) * a.ndim)
    ws = [cnt0, w['w_out'], w['g_ffn'], w['w_r_hi'], w['w_r_lo'], w['b_r']]
    return pl.pallas_call(
        _route_kernel, grid=(n // tm,),
        in_specs=[row(D_MODEL), row(MLA_HEADS * MLA_D_V), row(MOBA_WIDTH)] + [full(a) for a in ws],
        out_specs=[row(D_MODEL), row(D_MODEL), row(8), row(8), pl.BlockSpec((1, LANES), lambda i: (0, 0))],
        out_shape=[jax.ShapeDtypeStruct((n, D_MODEL), _f32), jax.ShapeDtypeStruct((n, D_MODEL), _f32),
                   jax.ShapeDtypeStruct((n, 8), jnp.int32), jax.ShapeDtypeStruct((n, 8), _f32),
                   jax.ShapeDtypeStruct((1, LANES), _f32)],
        scratch_shapes=[pltpu.VMEM((1, LANES), _f32)],
        compiler_params=pltpu.CompilerParams(dimension_semantics=("arbitrary",), vmem_limit_bytes=VMEM_LIMIT),
    )(x, o_mla, o_moba, *ws)


def _scatter_kernel(dest_ref, h_ref, xs_in_ref, xs_ref, sem):
    del xs_in_ref
    tm = h_ref.shape[0]

    def start(r, c):
        for k in range(2):
            pltpu.make_async_copy(h_ref.at[pl.ds(r, 1)], xs_ref.at[pl.ds(dest_ref[2 * r + k], 1)], sem).start()
        return c

    lax.fori_loop(0, tm, start, 0, unroll=8)
    for _ in range(2 * tm):
        pltpu.make_async_copy(h_ref.at[pl.ds(0, 1)], xs_ref.at[pl.ds(0, 1)], sem).wait()


def _scatter_rows(h2, dest, xs):
    n = h2.shape[0]
    tm = ROW_TILE
    return pl.pallas_call(
        _scatter_kernel, grid=(n // tm,),
        in_specs=[pl.BlockSpec((2 * tm,), lambda i: (i,), memory_space=pltpu.SMEM),
                  pl.BlockSpec((tm, D_MODEL), lambda i: (i, 0)),
                  pl.BlockSpec(memory_space=pl.ANY)],
        out_specs=pl.BlockSpec(memory_space=pl.ANY),
        out_shape=jax.ShapeDtypeStruct(xs.shape, xs.dtype),
        scratch_shapes=[pltpu.SemaphoreType.DMA(())],
        input_output_aliases={2: 0},
        compiler_params=pltpu.CompilerParams(dimension_semantics=("arbitrary",), has_side_effects=True),
    )(dest.reshape(-1), h2, xs)


def _expert_kernel(be_ref, na_ref, xs_ref, wg_ref, wu_ref, wd_ref, yb_ref, wg_sc, wu_sc, wd_sc):
    b = pl.program_id(0)

    @pl.when(b < na_ref[0])
    def _():
        prev = be_ref[jnp.maximum(b - 1, 0)]

        @pl.when((b == 0) | (be_ref[b] != prev))
        def _():
            wg_sc[...] = wg_ref[0].astype(_bf16)
            wu_sc[...] = wu_ref[0].astype(_bf16)
            wd_sc[...] = wd_ref[0].astype(_bf16)

        xi = xs_ref[...].astype(_bf16)
        g = _dot(xi, wg_sc[...])
        u = _dot(xi, wu_sc[...])
        a = g * (1.0 / (1.0 + jnp.exp(-g))) * u
        yb_ref[...] = _dot(a.astype(_bf16), wd_sc[...])

    @pl.when(b >= na_ref[0])
    def _():
        yb_ref[...] = jnp.zeros(yb_ref.shape, yb_ref.dtype)


def _experts(xs, blk_exp, n_active, w_eg, w_eu, w_ed):
    rows = xs.shape[0]
    r = MOE_ROWS
    nblk = rows // r
    blk = lambda b, be, na: (jnp.minimum(b, na[0] - 1), 0)
    wmap = lambda b, be, na: (be[jnp.minimum(b, na[0] - 1)], 0, 0)
    grid_spec = pltpu.PrefetchScalarGridSpec(
        num_scalar_prefetch=2, grid=(nblk,),
        in_specs=[pl.BlockSpec((r, D_MODEL), blk),
                  pl.BlockSpec((1, D_MODEL, D_EXPERT), wmap), pl.BlockSpec((1, D_MODEL, D_EXPERT), wmap),
                  pl.BlockSpec((1, D_EXPERT, D_MODEL), wmap)],
        out_specs=pl.BlockSpec((r, D_MODEL), lambda b, be, na: (b, 0)),
        scratch_shapes=[pltpu.VMEM((D_MODEL, D_EXPERT), _bf16), pltpu.VMEM((D_MODEL, D_EXPERT), _bf16),
                        pltpu.VMEM((D_EXPERT, D_MODEL), _bf16)])
    return pl.pallas_call(
        _expert_kernel, grid_spec=grid_spec, out_shape=jax.ShapeDtypeStruct((rows, D_MODEL), _f32),
        compiler_params=pltpu.CompilerParams(dimension_semantics=("arbitrary",), vmem_limit_bytes=VMEM_LIMIT),
    )(blk_exp, n_active, xs, w_eg, w_eu, w_ed)


def _combine_kernel(dest_ref, dnext_ref, yb_ref, x1_ref, rw_ref, p_ref, g_ple_ref, w_pg_ref, w_pp_ref, g_fin_ref,
                    y_ref, ybuf, sem, *, final_norm):
    tm = x1_ref.shape[0]
    i = pl.program_id(0)
    slot = i % 2

    def gather(d_ref, sl):
        def start(r, c):
            for k in range(2):
                pltpu.make_async_copy(yb_ref.at[pl.ds(d_ref[2 * r + k], 1)], ybuf.at[sl, k, pl.ds(r, 1)],
                                      sem.at[sl]).start()
            return c
        lax.fori_loop(0, tm, start, 0, unroll=8)

    @pl.when(i == 0)
    def _():
        gather(dest_ref, slot)

    @pl.when(i + 1 < pl.num_programs(0))
    def _():
        gather(dnext_ref, 1 - slot)

    ple = _dot(p_ref[...].astype(_bf16), w_pp_ref[...])
    for _ in range(2 * tm):
        pltpu.make_async_copy(yb_ref.at[pl.ds(0, 1)], ybuf.at[slot, 0, pl.ds(0, 1)], sem.at[slot]).wait()
    rw = rw_ref[...]
    x2 = x1_ref[...] + (ybuf[slot, 0] * rw[:, 0:1] + ybuf[slot, 1] * rw[:, 1:2])
    gate = _dot(_rms(x2, g_ple_ref[...]).astype(_bf16), w_pg_ref[...])
    x3 = x2 + ple * (1.0 / (1.0 + jnp.exp(-gate)))
    y_ref[...] = _rms(x3, g_fin_ref[...]) if final_norm else x3


def _combine(yb, dest, x1, rw, p, w, final_norm):
    n = x1.shape[0]
    tm = ROW_TILE
    nt = n // tm
    row = lambda wd: pl.BlockSpec((tm, wd), lambda i: (i, 0))
    full = lambda a: pl.BlockSpec(a.shape, lambda i: (0,) * a.ndim)
    ws = [w['g_ple'], w['w_ple_gate'], w['w_ple_proj'], w['g_final']]
    dflat = dest.reshape(-1)
    return pl.pallas_call(
        functools.partial(_combine_kernel, final_norm=final_norm), grid=(nt,),
        in_specs=[pl.BlockSpec((2 * tm,), lambda i: (i,), memory_space=pltpu.SMEM),
                  pl.BlockSpec((2 * tm,), lambda i: (jnp.minimum(i + 1, nt - 1),), memory_space=pltpu.SMEM),
                  pl.BlockSpec(memory_space=pl.ANY), row(D_MODEL), row(8), row(PLE_DIM)] + [full(a) for a in ws],
        out_specs=row(D_MODEL),
        out_shape=jax.ShapeDtypeStruct((n, D_MODEL), _f32),
        scratch_shapes=[pltpu.VMEM((2, 2, tm, D_MODEL), _f32), pltpu.SemaphoreType.DMA((2,))],
        compiler_params=pltpu.CompilerParams(dimension_semantics=("arbitrary",), vmem_limit_bytes=VMEM_LIMIT),
    )(dflat, dflat, yb, x1, rw, p, *ws)


def _post(groups, w, w_eg, w_eu, w_ed, final_norm=True):
    r = MOE_ROWS
    cnt = jnp.zeros((1, LANES), _f32)
    routed = []
    for x, o_mla, o_moba, _ in groups:
        x1, h2, ri, rw, cnt = _route(x, o_mla, o_moba, cnt, w)
        routed.append((x1, h2, ri, rw))
    counts = cnt[0, :MOE_EXPERTS].astype(jnp.int32)
    padded = (counts + r - 1) // r * r
    pad_end = jnp.cumsum(padded)
    pad_start = pad_end - padded
    n_all = sum(g[0].shape[0] for g in groups)
    nblk = -(-(2 * n_all) // r) + MOE_EXPERTS
    first_row = jnp.arange(nblk, dtype=jnp.int32)[:, None] * r
    blk_exp = jnp.minimum(jnp.sum(pad_end[None, :] <= first_row, axis=1), MOE_EXPERTS - 1).astype(jnp.int32)
    n_active = (pad_end[-1:] // r).astype(jnp.int32)
    xs = jnp.zeros((nblk * r, D_MODEL), _f32)
    dests = []
    for x1, h2, ri, rw in routed:
        dest = pad_start[ri[:, 0:2]] + ri[:, 2:4]
        xs = _scatter_rows(h2, dest, xs)
        dests.append(dest)
    yb = _experts(xs, blk_exp, n_active, w_eg, w_eu, w_ed)
    return [_combine(yb, dest, x1, rw, g[3], w, final_norm) for (x1, h2, ri, rw), dest, g in zip(routed, dests, groups)]


def _paged_step(n_steps_inner):
    step = pl.program_id(0) * n_steps_inner + pl.program_id(1)
    return step, step % 2


def _run_paged(copies, step, slot, n_inner, total=None):
    if total is None:
        total = pl.num_programs(0) * n_inner

    @pl.when(step == 0)
    def _():
        for cp in copies(0, 0, slot, False):
            cp.start()

    @pl.when(step + 1 < total)
    def _():
        nxt = step + 1
        for cp in copies(nxt // n_inner, nxt % n_inner, 1 - slot, False):
            cp.start()

    for cp in copies(0, 0, slot, True):
        cp.wait()


def _mla_sample_kernel(pt_ref, q_ref, knew_ref, ckv_hbm, kpet_hbm, o_ref, cbuf, pbuf, m_sc, l_sc, acc_sc, sem,
                       *, cp, nch, dt):
    c = pl.program_id(1)
    step, slot = _paged_step(nch)

    def copies(bb, cc, sl, fixed):
        out = []
        for j in range(cp):
            page = 0 if fixed else pt_ref[bb, cc * cp + j]
            out.append(pltpu.make_async_copy(ckv_hbm.at[page], cbuf.at[sl, j], sem.at[0, sl]))
            out.append(pltpu.make_async_copy(kpet_hbm.at[page], pbuf.at[sl, j], sem.at[1, sl]))
        return out

    _run_paged(copies, step, slot, nch)

    @pl.when(c == 0)
    def _():
        m_sc[...] = jnp.full(m_sc.shape, -jnp.inf, _f32)
        l_sc[...] = jnp.zeros(l_sc.shape, _f32)
        acc_sc[...] = jnp.zeros(acc_sc.shape, _f32)

    def update(s, v):
        m_prev = m_sc[...]
        m_new = jnp.maximum(m_prev, jnp.max(s, axis=-1, keepdims=True))
        alpha = jnp.exp(m_prev - m_new)
        p = jnp.exp(s - m_new)
        l_sc[...] = alpha * l_sc[...] + jnp.sum(p, axis=-1, keepdims=True)
        acc_sc[...] = alpha * acc_sc[...] + _dot(p.astype(_bf16), v)
        m_sc[...] = m_new

    q = q_ref[0]
    q_lat, q_pe = q[:, :MLA_KV_LORA], q[:, MLA_KV_LORA:]
    kc = cbuf[slot].reshape(cp * PAGE_SIZE, MLA_KV_LORA).astype(_bf16)
    s_pe = jnp.concatenate([_dot(q_pe, pbuf[slot, j].astype(_bf16)) for j in range(cp)], axis=1)
    update(_dot_nt(q_lat, kc) + s_pe, kc)

    @pl.when(c == nch - 1)
    def _():
        knew = knew_ref[0]
        s_new = _dot_nt(q, knew)
        t_row = lax.broadcasted_iota(jnp.int32, s_new.shape, 0) % dt
        col = lax.broadcasted_iota(jnp.int32, s_new.shape, 1)
        update(jnp.where(col <= t_row, s_new, -jnp.inf), knew[:, :MLA_KV_LORA])
        o_ref[0] = acc_sc[...] / l_sc[...]


def _mla_sample(q_rows, k_new, ckv_pool, kpet_pool, page_table, dt):
    db, npg = page_table.shape
    cp = min(MLA_PAGES, npg)
    assert npg % cp == 0
    nch = npg // cp
    rows = MLA_HEADS * dt
    grid_spec = pltpu.PrefetchScalarGridSpec(
        num_scalar_prefetch=1, grid=(db, nch),
        in_specs=[pl.BlockSpec((1, rows, MLA_QK), lambda b, c, pt: (b, 0, 0)),
                  pl.BlockSpec((1,) + k_new.shape[1:], lambda b, c, pt: (b, 0, 0)),
                  pl.BlockSpec(memory_space=pl.ANY), pl.BlockSpec(memory_space=pl.ANY)],
        out_specs=pl.BlockSpec((1, rows, MLA_KV_LORA), lambda b, c, pt: (b, 0, 0)),
        scratch_shapes=[pltpu.VMEM((2, cp, PAGE_SIZE, MLA_KV_LORA), _f32),
                        pltpu.VMEM((2, cp, MLA_D_ROPE, PAGE_SIZE), _f32),
                        pltpu.VMEM((rows, 1), _f32), pltpu.VMEM((rows, 1), _f32),
                        pltpu.VMEM((rows, MLA_KV_LORA), _f32), pltpu.SemaphoreType.DMA((2, 2))])
    return pl.pallas_call(
        functools.partial(_mla_sample_kernel, cp=cp, nch=nch, dt=dt), grid_spec=grid_spec,
        out_shape=jax.ShapeDtypeStruct((db, rows, MLA_KV_LORA), _f32),
        compiler_params=pltpu.CompilerParams(dimension_semantics=("arbitrary", "arbitrary"),
                                             vmem_limit_bytes=VMEM_LIMIT),
    )(page_table, q_rows, k_new, ckv_pool, kpet_pool)


def _uv_kernel(o_ref, w_ref, out_ref):
    for p in range(MLA_HEADS // 2):
        pair = o_ref[:, 2 * p * MLA_KV_LORA:(2 * p + 2) * MLA_KV_LORA].astype(_bf16)
        out_ref[:, p * LANES:(p + 1) * LANES] = _dot(pair, w_ref[p]).astype(out_ref.dtype)


def _uv_project(o_tok, w_uvp):
    n = o_tok.shape[0]
    tm = ROW_TILE
    return pl.pallas_call(
        _uv_kernel, grid=(n // tm,),
        in_specs=[pl.BlockSpec((tm, MLA_HEADS * MLA_KV_LORA), lambda i: (i, 0)),
                  pl.BlockSpec(w_uvp.shape, lambda i: (0, 0, 0))],
        out_specs=pl.BlockSpec((tm, MLA_HEADS * MLA_D_V), lambda i: (i, 0)),
        out_shape=jax.ShapeDtypeStruct((n, MLA_HEADS * MLA_D_V), _bf16),
    )(o_tok, w_uvp)


def _top_idx(gate, lane, n_valid):
    g = jnp.where(lane < n_valid, gate, -jnp.inf)
    out = []
    for _ in range(MOBA_TOPK):
        mx = jnp.max(g, axis=-1, keepdims=True)
        idx = jnp.min(jnp.where(g == mx, lane, jnp.int32(2 ** 30)), axis=-1, keepdims=True)
        out.append(idx)
        g = jnp.where(lane == idx, -jnp.inf, g)
    return out


def _select_plan(db, npg, steps_available):
    ppb = MOBA_BLOCK // PAGE_SIZE
    nb = npg // ppb
    assert npg % ppb == 0 and MOBA_TOPK <= nb <= LANES
    cp = min(MOBA_PAGES, npg)
    while npg % cp or cp % ppb or db * (npg // cp) > steps_available:
        cp += ppb
        assert cp <= npg, "not enough host-kernel steps to stream the MoBA key cache"
    return cp, npg // cp, nb


def _select_fetch(step, total, pt_ref, kt_hbm, kbuf, sem, *, cp, nch):
    def copies(bb, cc, sl, fixed):
        return [pltpu.make_async_copy(kt_hbm.at[0 if fixed else pt_ref[bb, cc * cp + j]], kbuf.at[sl, j], sem.at[sl])
                for j in range(cp)]

    _run_paged(copies, step, step % 2, nch, total)


def _select_accumulate(step, kbuf, ksum_sc, *, cp, nch):
    c = step % nch
    slot = step % 2
    ppb = MOBA_BLOCK // PAGE_SIZE
    lane = lax.broadcasted_iota(jnp.int32, (MOBA_HEAD_DIM, LANES), 1)
    for h in range(MOBA_HEADS):
        acc = jnp.where(c == 0, 0.0, ksum_sc[h])
        for jb in range(cp // ppb):
            blk = kbuf[slot, ppb * jb, h]
            for pg in range(1, ppb):
                blk = blk + kbuf[slot, ppb * jb + pg, h]
            red = jnp.sum(blk, axis=-1, keepdims=True)
            acc = jnp.where(lane == c * (cp // ppb) + jb, red, acc)
        ksum_sc[h] = acc


def _select_finish(qt_ref, idx_ref, ksum_sc, *, dt, nb):
    lane_t = lax.broadcasted_iota(jnp.int32, (dt, LANES), 1)
    for h in range(MOBA_HEADS):
        km = ksum_sc[h] * (1.0 / MOBA_BLOCK)
        qt = qt_ref[0, h]
        gate = jnp.concatenate([jnp.sum(km * qt[:, t:t + 1], axis=0, keepdims=True) for t in range(dt)], axis=0)
        i0, i1, i2 = _top_idx(gate, lane_t, nb)
        idx_ref[0, h] = jnp.where(lane_t == 0, i0, jnp.where(lane_t == 1, i1, i2))


def _moba_sample_kernel(pt_ref, sel_ref, qt_ref, knt_ref, vnt_ref, kt_hbm, vt_hbm, o_ref, kbuf, vbuf, sem,
                        *, hg, nhg, dt):
    g = pl.program_id(1)
    step, slot = _paged_step(nhg)
    ppb = MOBA_BLOCK // PAGE_SIZE
    per_q = MOBA_TOPK * ppb

    def copies(bb, gg, sl, fixed):
        out = []
        for hl in range(hg):
            h = 0 if fixed else gg * hg + hl
            for t in range(dt):
                for k in range(MOBA_TOPK):
                    blk = 0 if fixed else sel_ref[((bb * MOBA_HEADS + h) * dt + t) * MOBA_TOPK + k]
                    for pg in range(ppb):
                        page = 0 if fixed else pt_ref[bb, blk * ppb + pg]
                        i = (hl * dt + t) * per_q + k * ppb + pg
                        out.append(pltpu.make_async_copy(kt_hbm.at[page, h], kbuf.at[sl, i], sem.at[0, sl]))
                        out.append(pltpu.make_async_copy(vt_hbm.at[page, h], vbuf.at[sl, i], sem.at[1, sl]))
        return out

    _run_paged(copies, step, slot, nhg)

    lane_t = lax.broadcasted_iota(jnp.int32, (1, dt), 1)
    lane_o = lax.broadcasted_iota(jnp.int32, (MOBA_HEAD_DIM, dt), 1)
    for hl in range(hg):
        qt, knt, vnt = qt_ref[0, hl], knt_ref[0, hl], vnt_ref[0, hl]
        o_cols = jnp.zeros((MOBA_HEAD_DIM, dt), _f32)
        for t in range(dt):
            qcol = qt[:, t:t + 1]
            base = (hl * dt + t) * per_q
            s_sel = [jnp.sum(kbuf[slot, base + r] * qcol, axis=0, keepdims=True) for r in range(per_q)]
            s_own = jnp.where(lane_t <= t, jnp.sum(knt * qcol, axis=0, keepdims=True), -jnp.inf)
            m_row = s_sel[0]
            for r in range(1, per_q):
                m_row = jnp.maximum(m_row, s_sel[r])
            m = jnp.maximum(jnp.max(m_row, axis=-1, keepdims=True), jnp.max(s_own, axis=-1, keepdims=True))
            e_sel = [jnp.exp(s - m) for s in s_sel]
            e_own = jnp.exp(s_own - m)
            e_row = e_sel[0]
            acc = vbuf[slot, base] * e_sel[0]
            for r in range(1, per_q):
                e_row = e_row + e_sel[r]
                acc = acc + vbuf[slot, base + r] * e_sel[r]
            l = jnp.sum(e_row, axis=-1, keepdims=True) + jnp.sum(e_own, axis=-1, keepdims=True)
            o_col = (jnp.sum(acc, axis=-1, keepdims=True) + jnp.sum(vnt * e_own, axis=-1, keepdims=True)) / l
            o_cols = jnp.where(lane_o == t, o_col, o_cols)
        o_ref[0, hl] = o_cols


def _moba_sample(qt, knt, vnt, sel, kt_pool, vt_pool, page_table, dt):
    db, npg = page_table.shape
    hg = MOBA_HEAD_GROUP
    nhg = MOBA_HEADS // hg
    nslab = hg * dt * MOBA_TOPK * (MOBA_BLOCK // PAGE_SIZE)
    col = pl.BlockSpec((1, hg, MOBA_HEAD_DIM, dt), lambda b, g, pt, sel: (b, g, 0, 0))
    grid_spec = pltpu.PrefetchScalarGridSpec(
        num_scalar_prefetch=2, grid=(db, nhg),
        in_specs=[col, col, col, pl.BlockSpec(memory_space=pl.ANY), pl.BlockSpec(memory_space=pl.ANY)],
        out_specs=col,
        scratch_shapes=[pltpu.VMEM((2, nslab, MOBA_HEAD_DIM, PAGE_SIZE), _f32),
                        pltpu.VMEM((2, nslab, MOBA_HEAD_DIM, PAGE_SIZE), _f32), pltpu.SemaphoreType.DMA((2, 2))])
    return pl.pallas_call(
        functools.partial(_moba_sample_kernel, hg=hg, nhg=nhg, dt=dt), grid_spec=grid_spec,
        out_shape=jax.ShapeDtypeStruct((db, MOBA_HEADS, MOBA_HEAD_DIM, dt), _f32),
        compiler_params=pltpu.CompilerParams(dimension_semantics=("arbitrary", "arbitrary"),
                                             vmem_limit_bytes=VMEM_LIMIT),
    )(page_table, sel, qt, knt, vnt, kt_pool, vt_pool)


def _head_cols(a, db, dt):
    return a.reshape(db, dt, MOBA_HEADS, MOBA_HEAD_DIM).transpose(0, 2, 3, 1)


def _sample_attention(qcat_s, ckv_s, kpe_s, qt, kb_s, vb_s, sel, cache_ckv, kpet_pool, kt_pool, vt_pool, page_table,
                      w_uvp, db, dt):
    n_s = db * dt
    q_rows = qcat_s.reshape(MLA_HEADS, db, dt, MLA_QK).transpose(1, 0, 2, 3).reshape(db, MLA_HEADS * dt, MLA_QK)
    k_new = jnp.concatenate([ckv_s, kpe_s], axis=1).astype(_bf16).reshape(db, dt, MLA_QK)
    assert dt <= 16
    k_new = jnp.pad(k_new, ((0, 0), (0, 16 - dt), (0, 0)))
    o_lat = _mla_sample(q_rows, k_new, cache_ckv, kpet_pool, page_table, dt)
    o_tok = o_lat.reshape(db, MLA_HEADS, dt, MLA_KV_LORA).transpose(0, 2, 1, 3).reshape(n_s, -1)
    o_mla = _uv_project(o_tok, w_uvp)
    knt, vnt = _head_cols(kb_s, db, dt), _head_cols(vb_s, db, dt)
    ot = _moba_sample(qt, knt, vnt, sel[..., :MOBA_TOPK].reshape(-1), kt_pool, vt_pool, page_table, dt)
    o_moba = ot.transpose(0, 3, 1, 2).reshape(n_s, MOBA_WIDTH).astype(_bf16)
    return o_mla, o_moba


def kernel(x_prompt, x_sample, cache_ckv, cache_kpe, cache_k, cache_v, page_table, p_prompt, p_sample, g_attn, w_in,
           g_q, w_uq, g_kv, w_uk, w_uv, w_out, g_ffn, w_group_router, b_group_router, w_expert_router,
           b_expert_router, w_exp_gate, w_exp_up, w_exp_down, g_ple, w_ple_gate, w_ple_proj, g_final):
    bsz, seq, _ = x_prompt.shape
    db, dt, _ = x_sample.shape
    depth = g_attn.shape[0]
    n_p, n_s = bsz * seq, db * dt
    past = page_table.shape[1] * PAGE_SIZE
    assert ROW_TILE % dt == 0 and n_s % ROW_TILE == 0
    pos_p = jnp.arange(seq, dtype=jnp.int32)
    pos_s = past + jnp.arange(ROW_TILE, dtype=jnp.int32) % dt
    xp = x_prompt.reshape(n_p, D_MODEL)
    xs = x_sample.reshape(n_s, D_MODEL)
    outs_p, outs_s = [], []
    for i in range(depth):
        wa = _prep_attn_weights(g_attn[i], w_in[i], g_q[i], w_uq[i], g_kv[i], w_uk[i], w_uv[i])
        wp = _prep_post_weights(w_out[i], g_ffn[i], w_group_router[i], b_group_router[i], w_expert_router[i],
                                b_expert_router[i], g_ple[i], w_ple_gate[i], w_ple_proj[i], g_final)
        kpet_pool = jnp.transpose(cache_kpe[i], (0, 2, 1))
        kt_pool = jnp.transpose(cache_k[i], (0, 2, 3, 1))
        vt_pool = jnp.transpose(cache_v[i], (0, 2, 3, 1))
        ckv, kpe, kb, vb, qcat, kcat, qbh, kaug, vbh, kmean = _project(xp, pos_p, wa, sample=False)
        ckv_s, kpe_s, kb_s, vb_s, qcat_s, qb_s = _project(xs, pos_s, wa, sample=True)
        qt = _head_cols(qb_s, db, dt)
        o_mla, sel = _mla_prompt(qcat, kcat, wa['w_uvp'], bsz, seq, qt, kt_pool, page_table)
        o_mla = o_mla.reshape(n_p, MLA_HEADS * MLA_D_V)
        o_moba = _moba_prompt(qbh, kaug, vbh, kmean, bsz, seq).reshape(n_p, MOBA_WIDTH)
        outs_p.append((ckv, kpe, kb, vb))
        o_mla_s, o_moba_s = _sample_attention(qcat_s, ckv_s, kpe_s, qt, kb_s, vb_s, sel, cache_ckv[i], kpet_pool,
                                              kt_pool, vt_pool, page_table, wa['w_uvp'], db, dt)
        outs_s.append((ckv_s, kpe_s, kb_s, vb_s))
        xp, xs = _post([(xp, o_mla, o_moba, p_prompt[i].reshape(n_p, PLE_DIM)),
                        (xs, o_mla_s, o_moba_s, p_sample[i].reshape(n_s, PLE_DIM))],
                       wp, w_exp_gate[i], w_exp_up[i], w_exp_down[i], final_norm=(i == depth - 1))
    stack = lambda outs, j, shp: jnp.stack([o[j].reshape(shp) for o in outs])
    hs = (MOBA_HEADS, MOBA_HEAD_DIM)
    return (xp.reshape(bsz, seq, D_MODEL), xs.reshape(db, dt, D_MODEL),
            stack(outs_p, 0, (bsz, seq, MLA_KV_LORA)), stack(outs_p, 1, (bsz, seq, MLA_D_ROPE)),
            stack(outs_p, 2, (bsz, seq) + hs), stack(outs_p, 3, (bsz, seq) + hs),
            stack(outs_s, 0, (db, dt, MLA_KV_LORA)), stack(outs_s, 1, (db, dt, MLA_D_ROPE)),
            stack(outs_s, 2, (db, dt) + hs), stack(outs_s, 3, (db, dt) + hs))
```
